```python
import jax, jax.numpy as jnp
from jax import lax
import numpy as np

D_MODEL = 2048
BATCH = 4
SEQ = 4096
DEPTH = 1
DEC_BATCH = 16
DEC_SEQ = 32
PAST_LEN = 1024

CHUNK = 64
MEM_LEN = 256
RET_HEADS = 8
RET_HEAD_DIM = 128
RET_WIDTH = RET_HEADS * RET_HEAD_DIM
GMLP_GROUPS = 4
GMLP_GROUP_DIM = 128
GMLP_WIDTH = GMLP_GROUPS * GMLP_GROUP_DIM
GMLP_CHUNK = 128
XA_HEADS = 4
XA_HEAD_DIM = 128
XA_WIDTH = XA_HEADS * XA_HEAD_DIM
MIX_WIDTH = RET_WIDTH + GMLP_WIDTH + XA_WIDTH
IN_WIDTH = 4 * RET_WIDTH + 3 * GMLP_WIDTH + 2 * XA_WIDTH
ROPE_BASE = 10000.0
EPS = 1e-6

kernel_name = "hybrid_retention_gmlp_memxattn_stream_step"


def rmsnorm(x, g):
    x32 = x.astype(jnp.float32)
    y = x32 * lax.rsqrt(jnp.mean(x32 * x32, axis=-1, keepdims=True) + EPS)
    return (y * g.astype(jnp.float32)).astype(x.dtype)


def layernorm(x, g):
    x32 = x.astype(jnp.float32)
    mu = jnp.mean(x32, axis=-1, keepdims=True)
    var = jnp.mean(jnp.square(x32 - mu), axis=-1, keepdims=True)
    return ((x32 - mu) * lax.rsqrt(var + EPS) * g.astype(jnp.float32)).astype(x.dtype)


def rope(x, pos):
    dh = x.shape[-1]
    inv_freq = ROPE_BASE ** (-jnp.arange(0, dh, 2, dtype=jnp.float32) / dh)
    ang = pos.astype(jnp.float32)[:, None] * inv_freq[None, :]
    cos = jnp.cos(ang)[None, :, None, :]
    sin = jnp.sin(ang)[None, :, None, :]
    x32 = x.astype(jnp.float32)
    x1, x2 = x32[..., : dh // 2], x32[..., dh // 2:]
    return jnp.concatenate([x1 * cos - x2 * sin, x1 * sin + x2 * cos], axis=-1)


def retention_block(q, k, v, S0, log_gamma):
    L = q.shape[1]
    idx = jnp.arange(L, dtype=jnp.float32)
    diff = idx[:, None] - idx[None, :]
    causal = diff >= 0
    decay = jnp.where(causal[None], jnp.exp(log_gamma[:, None, None] * jnp.where(causal, diff, 0.0)[None]), 0.0)
    scores = jnp.einsum('bihd,bjhd->bhij', q, k) * decay[None]
    o_intra = jnp.einsum('bhij,bjhe->bihe', scores, v)
    q_decay = jnp.exp(log_gamma[None, :] * (idx[:, None] + 1.0))
    o_cross = jnp.einsum('bihd,bhde->bihe', q, S0) * q_decay[None, :, :, None]
    k_decay = jnp.exp(log_gamma[None, :] * (L - 1.0 - idx[:, None]))
    S_new = jnp.exp(log_gamma * L)[None, :, None, None] * S0 + jnp.einsum(
        'bjhd,bjhe->bhde', k * k_decay[None, :, :, None], v)
    return o_intra + o_cross, S_new


def retention_forward(q, k, v, S0, log_gamma):
    B, L, H, Dh = q.shape
    blk = CHUNK if L % CHUNK == 0 else L
    n = L // blk

    def to_blocks(t):
        return jnp.moveaxis(t.reshape(B, n, blk, H, t.shape[-1]), 1, 0)

    def step(S, qkv):
        qb, kb, vb = qkv
        o, S_next = retention_block(qb, kb, vb, S, log_gamma)
        return S_next, o

    S_final, o = lax.scan(step, S0.astype(jnp.float32), (to_blocks(q), to_blocks(k), to_blocks(v)))
    o = jnp.moveaxis(o, 0, 1).reshape(B, L, H, v.shape[-1])
    return o, S_final


def head_groupnorm(o, g):
    mu = jnp.mean(o, axis=-1, keepdims=True)
    var = jnp.mean(jnp.square(o - mu), axis=-1, keepdims=True)
    B, L, H, D = o.shape
    return ((o - mu) * lax.rsqrt(var + EPS)).reshape(B, L, H * D) * g.astype(jnp.float32)


def memory_kv(mem, g_mem, w_mem_kv):
    B, M, _ = mem.shape
    kv = rmsnorm(mem, g_mem) @ w_mem_kv
    mk, mv = jnp.split(kv, 2, axis=-1)
    return mk.reshape(B, M, XA_HEADS, XA_HEAD_DIM), mv.reshape(B, M, XA_HEADS, XA_HEAD_DIM)


def hybrid_layer(x, pos, S0, mem_k, mem_v, g_norm, w_in, g_ret, g_gmlp, w_s, b_s, w_out):
    B, L, _ = x.shape
    h = rmsnorm(x, g_norm)
    proj = h @ w_in
    cuts = [RET_WIDTH, 2 * RET_WIDTH, 3 * RET_WIDTH, 4 * RET_WIDTH,
            4 * RET_WIDTH + GMLP_WIDTH, 4 * RET_WIDTH + 2 * GMLP_WIDTH, 4 * RET_WIDTH + 3 * GMLP_WIDTH,
            4 * RET_WIDTH + 3 * GMLP_WIDTH + XA_WIDTH]
    rq, rk, rv, rg, gu, gv, gg, aq, ag = jnp.split(proj, cuts, axis=-1)

    rq = rope(rq.reshape(B, L, RET_HEADS, RET_HEAD_DIM), pos)
    rk = rope(rk.reshape(B, L, RET_HEADS, RET_HEAD_DIM), pos) * (RET_HEAD_DIM ** -0.5)
    rv = rv.reshape(B, L, RET_HEADS, RET_HEAD_DIM).astype(jnp.float32)
    log_gamma = jnp.log(1.0 - 2.0 ** (-5.0 - jnp.arange(RET_HEADS, dtype=jnp.float32)))
    ro, S_new = retention_forward(rq, rk, rv, S0, log_gamma)
    ro = head_groupnorm(ro, g_ret).astype(x.dtype) * jax.nn.silu(rg)

    gv_n = layernorm(gv, g_gmlp)
    blk = GMLP_CHUNK if L % GMLP_CHUNK == 0 else L
    n = L // blk
    vb = gv_n.reshape(B, n, blk, GMLP_GROUPS, GMLP_GROUP_DIM)
    w_mask = jnp.tril(w_s[:, :blk, :blk])
    s = jnp.einsum('gij,bnjgc->bnigc', w_mask, vb) + b_s[:, :blk].T[None, None, :, :, None]
    go = gu * s.reshape(B, L, GMLP_WIDTH) * jax.nn.silu(gg)

    aq = aq.reshape(B, L, XA_HEADS, XA_HEAD_DIM).astype(jnp.float32)
    sc = jnp.einsum('blhd,bmhd->bhlm', aq, mem_k.astype(jnp.float32)) * (XA_HEAD_DIM ** -0.5)
    p = jax.nn.softmax(sc, axis=-1)
    ao = jnp.einsum('bhlm,bmhd->blhd', p, mem_v.astype(jnp.float32)).reshape(B, L, XA_WIDTH)
    ao = ao.astype(x.dtype) * jax.nn.silu(ag)

    out = x + jnp.concatenate([ro, go, ao], axis=-1) @ w_out
    return out, S_new, gv_n


def setup_inputs(seed: int = 0) -> dict:
    key = jax.random.key(seed)
    ks = jax.random.split(key, 20)
    f32 = jnp.float32
    nrm = lambda k, shape, scale: jax.random.normal(k, shape, f32) * scale
    return {
        "x_prompt": nrm(ks[0], (BATCH, SEQ, D_MODEL), 1.0),
        "x_sample": nrm(ks[1], (DEC_BATCH, DEC_SEQ, D_MODEL), 1.0),
        "mem_prompt": nrm(ks[2], (BATCH, MEM_LEN, D_MODEL), 1.0),
        "state_ret": nrm(ks[3], (DEPTH, DEC_BATCH, RET_HEADS, RET_HEAD_DIM, RET_HEAD_DIM), 0.5),
        "cache_mem_k": nrm(ks[4], (DEPTH, DEC_BATCH, MEM_LEN, XA_HEADS, XA_HEAD_DIM), 1.0),
        "cache_mem_v": nrm(ks[5], (DEPTH, DEC_BATCH, MEM_LEN, XA_HEADS, XA_HEAD_DIM), 1.0),
        "g_norm": 1.0 + nrm(ks[6], (DEPTH, D_MODEL), 0.02),
        "w_in": nrm(ks[7], (DEPTH, D_MODEL, IN_WIDTH), D_MODEL ** -0.5),
        "g_ret": 1.0 + nrm(ks[8], (DEPTH, RET_WIDTH), 0.02),
        "g_gmlp": 1.0 + nrm(ks[9], (DEPTH, GMLP_WIDTH), 0.02),
        "w_s": nrm(ks[10], (DEPTH, GMLP_GROUPS, GMLP_CHUNK, GMLP_CHUNK), GMLP_CHUNK ** -0.5),
        "b_s": 1.0 + nrm(ks[11], (DEPTH, GMLP_GROUPS, GMLP_CHUNK), 0.02),
        "g_mem": 1.0 + nrm(ks[12], (DEPTH, D_MODEL), 0.02),
        "w_mem_kv": nrm(ks[13], (DEPTH, D_MODEL, 2 * XA_WIDTH), D_MODEL ** -0.5),
        "w_out": nrm(ks[14], (DEPTH, MIX_WIDTH, D_MODEL), MIX_WIDTH ** -0.5),
        "g_final": 1.0 + nrm(ks[15], (D_MODEL,), 0.02),
    }


def reference(x_prompt, x_sample, mem_prompt, state_ret, cache_mem_k, cache_mem_v,
              g_norm, w_in, g_ret, g_gmlp, w_s, b_s, g_mem, w_mem_kv, w_out, g_final):
    b_p, l_p, _ = x_prompt.shape
    l_s = x_sample.shape[1]
    pos_p = jnp.arange(l_p, dtype=jnp.int32)
    pos_s = PAST_LEN + jnp.arange(l_s, dtype=jnp.int32)
    hp, hs = x_prompt, x_sample
    sp_list, mk_list, mv_list, ss_list, vs_list = [], [], [], [], []
    for l in range(DEPTH):
        mk, mv = memory_kv(mem_prompt, g_mem[l], w_mem_kv[l])
        S0p = jnp.zeros((b_p, RET_HEADS, RET_HEAD_DIM, RET_HEAD_DIM), jnp.float32)
        hp, Sp, _ = hybrid_layer(hp, pos_p, S0p, mk, mv, g_norm[l], w_in[l], g_ret[l],
                                 g_gmlp[l], w_s[l], b_s[l], w_out[l])
        hs, Ss, vs = hybrid_layer(hs, pos_s, state_ret[l], cache_mem_k[l], cache_mem_v[l],
                                  g_norm[l], w_in[l], g_ret[l], g_gmlp[l], w_s[l], b_s[l], w_out[l])
        sp_list.append(Sp)
        mk_list.append(mk)
        mv_list.append(mv)
        ss_list.append(Ss)
        vs_list.append(vs)
    y_prompt = rmsnorm(hp, g_final)
    y_sample = rmsnorm(hs, g_final)
    return (y_prompt, y_sample, jnp.stack(sp_list), jnp.stack(mk_list), jnp.stack(mv_list),
            jnp.stack(ss_list), jnp.stack(vs_list))
```

```python
import functools

import jax
import jax.numpy as jnp
from jax import lax
from jax.experimental import pallas as pl
from jax.experimental.pallas import tpu as pltpu

D_MODEL = 2048
PAST_LEN = 1024
MEM_LEN = 256
RET_HEADS = 8
HEAD_DIM = 128
RET_WIDTH = RET_HEADS * HEAD_DIM
GMLP_GROUPS = 4
GMLP_WIDTH = GMLP_GROUPS * HEAD_DIM
GMLP_CHUNK = 128
XA_HEADS = 4
XA_WIDTH = XA_HEADS * HEAD_DIM
MIX_WIDTH = RET_WIDTH + GMLP_WIDTH + XA_WIDTH
ROPE_BASE = 10000.0
EPS = 1e-6

_RQ, _RK, _RV, _RG = 0, RET_WIDTH, 2 * RET_WIDTH, 3 * RET_WIDTH
_GU = 4 * RET_WIDTH
_GV = _GU + GMLP_WIDTH
_GG = _GV + GMLP_WIDTH
_AQ = _GG + GMLP_WIDTH
_AG = _AQ + XA_WIDTH

_V7X_VMEM_LIMIT_BYTES = 60 * 1024 * 1024

_BF16 = jnp.bfloat16
_F32 = jnp.float32


def _dot(a, b):
    return jnp.dot(a, b, preferred_element_type=_F32)


def _dot_nt(a, b):
    return lax.dot_general(a, b, (((1,), (1,)), ((), ())), preferred_element_type=_F32)


def _dot_tn(a, b):
    return lax.dot_general(a, b, (((0,), (0,)), ((), ())), preferred_element_type=_F32)


def _silu(x):
    return x / (1.0 + jnp.exp(-x))


def _rms_scale(x):
    return x * lax.rsqrt(jnp.mean(x * x, axis=-1, keepdims=True) + EPS)


def _center_scale(x):
    mu = jnp.mean(x, axis=-1, keepdims=True)
    d = x - mu
    return d * lax.rsqrt(jnp.mean(d * d, axis=-1, keepdims=True) + EPS)


def _memkv_kernel(mem_ref, g_ref, w_ref, k_ref, v_ref, k16_ref, v16_ref):
    n = (_rms_scale(mem_ref[0]) * g_ref[...]).astype(_BF16)
    kv = _dot(n, w_ref[...])
    k = kv[:, :XA_WIDTH]
    v = kv[:, XA_WIDTH:]
    k_ref[0] = k
    v_ref[0] = v
    k16_ref[0] = k.astype(_BF16)
    v16_ref[0] = v.astype(_BF16)


def _layer_kernel(*refs, ns, ts, lb, gblk, has_state, out_gvn):
    it = iter(refs)
    x_ref, cos_ref, sin_ref, mk_ref, mv_ref = (next(it) for _ in range(5))
    s0_ref = next(it) if has_state else None
    (gnorm_ref, win_ref, gret_ref, ggm_ref, ws_ref, bs_ref, wout_ref, gfin_ref,
     dmask_ref, qdec_ref, kdec_ref, sdec_ref) = (next(it) for _ in range(12))
    y_ref, s_ref = next(it), next(it)
    gvn_ref = next(it) if out_gvn else None
    h_s, mix_s = next(it), next(it)

    @pl.when(pl.program_id(1) == 0)
    def _():
        if has_state:
            s_ref[...] = s0_ref[...]
        else:
            s_ref[...] = jnp.zeros_like(s_ref)

    h_s[...] = (_rms_scale(x_ref[0]) * gnorm_ref[...]).astype(_BF16)

    def proj(c0, width):
        return _dot(h_s[...], win_ref[:, c0:c0 + width])

    cos = cos_ref[...]
    sin = sin_ref[...]

    def rope(u):
        return u * cos + pltpu.roll(u, HEAD_DIM // 2, 1) * sin

    for p in range(RET_HEADS // 2):
        c0 = 2 * HEAD_DIM * p
        q2, k2, v2, g2 = (proj(sec + c0, 2 * HEAD_DIM) for sec in (_RQ, _RK, _RV, _RG))
        for hh in range(2):
            head = 2 * p + hh
            hs = slice(hh * HEAD_DIM, (hh + 1) * HEAD_DIM)
            cs = slice(head * HEAD_DIM, (head + 1) * HEAD_DIM)
            qh = rope(q2[:, hs])
            kh = rope(k2[:, hs]) * (HEAD_DIM ** -0.5)
            vh = v2[:, hs].astype(_BF16)
            gh = g2[:, hs]
            for s in range(ns):
                for bi in range(ts // lb):
                    rs = slice(s * ts + bi * lb, s * ts + (bi + 1) * lb)
                    qb = qh[rs].astype(_BF16)
                    kb = kh[rs]
                    vb = vh[rs]
                    state = s_ref[s, head]
                    sc = _dot_nt(qb, kb.astype(_BF16)) * dmask_ref[head]
                    o = _dot(sc.astype(_BF16), vb) + _dot(qb, state.astype(_BF16)) * qdec_ref[head]
                    s_ref[s, head] = state * sdec_ref[head] + _dot_tn(
                        (kb * kdec_ref[head]).astype(_BF16), vb)
                    on = _center_scale(o) * gret_ref[:, cs]
                    mix_s[rs, cs] = (on * _silu(gh[rs])).astype(_BF16)

    gu = proj(_GU, GMLP_WIDTH)
    gvn = _center_scale(proj(_GV, GMLP_WIDTH)) * ggm_ref[...]
    gg = proj(_GG, GMLP_WIDTH)
    if out_gvn:
        gvn_ref[0] = gvn
    row = lax.broadcasted_iota(jnp.int32, (gblk, gblk), 0)
    col = lax.broadcasted_iota(jnp.int32, (gblk, gblk), 1)
    for g in range(GMLP_GROUPS):
        hs = slice(g * HEAD_DIM, (g + 1) * HEAD_DIM)
        cs = slice(RET_WIDTH + g * HEAD_DIM, RET_WIDTH + (g + 1) * HEAD_DIM)
        wm = jnp.where(row >= col, ws_ref[g], 0.0).astype(_BF16)
        for c in range(ns * ts // gblk):
            rs = slice(c * gblk, (c + 1) * gblk)
            sg = _dot(wm, gvn[rs, hs].astype(_BF16)) + bs_ref[g]
            mix_s[rs, cs] = (gu[rs, hs] * sg * _silu(gg[rs, hs])).astype(_BF16)

    aq = proj(_AQ, XA_WIDTH)
    ag = proj(_AG, XA_WIDTH)
    for s in range(ns):
        rs = slice(s * ts, (s + 1) * ts)
        for hd in range(XA_HEADS):
            hs = slice(hd * HEAD_DIM, (hd + 1) * HEAD_DIM)
            cs = slice(RET_WIDTH + GMLP_WIDTH + hd * HEAD_DIM,
                       RET_WIDTH + GMLP_WIDTH + (hd + 1) * HEAD_DIM)
            sc = _dot_nt(aq[rs, hs].astype(_BF16), mk_ref[s, :, hs]) * (HEAD_DIM ** -0.5)
            e = jnp.exp(sc - jnp.max(sc, axis=-1, keepdims=True))
            ao = _dot(e.astype(_BF16), mv_ref[s, :, hs]) / jnp.sum(e, axis=-1, keepdims=True)
            mix_s[rs, cs] = (ao * _silu(ag[rs, hs])).astype(_BF16)

    y = x_ref[0] + _dot(mix_s[...], wout_ref[...])
    y_ref[0] = _rms_scale(y) * gfin_ref[...]


def _const_spec(shape):
    return pl.BlockSpec(shape, lambda *_: (0,) * len(shape), pipeline_mode=pl.Buffered(1))


def _rope_tables(pos):
    inv_freq = ROPE_BASE ** (-jnp.arange(0, HEAD_DIM, 2, dtype=_F32) / HEAD_DIM)
    ang = pos.astype(_F32)[:, None] * inv_freq[None, :]
    cos, sin = jnp.cos(ang), jnp.sin(ang)
    return jnp.concatenate([cos, cos], axis=-1), jnp.concatenate([-sin, sin], axis=-1)


def _decay_tables(lb):
    log_gamma = jnp.log(1.0 - 2.0 ** (-5.0 - jnp.arange(RET_HEADS, dtype=_F32)))
    idx = jnp.arange(lb, dtype=_F32)
    diff = idx[:, None] - idx[None, :]
    causal = diff >= 0
    dmask = jnp.where(causal[None], jnp.exp(log_gamma[:, None, None] * jnp.where(causal, diff, 0.0)[None]), 0.0)
    qdec = jnp.exp(log_gamma[:, None] * (idx[None, :] + 1.0))
    kdec = jnp.exp(log_gamma[:, None] * (lb - 1.0 - idx[None, :]))
    sdec = jnp.exp(log_gamma * lb)
    bcast = lambda a: jnp.broadcast_to(a[..., None], a.shape + (HEAD_DIM,))
    return dmask, bcast(qdec), bcast(kdec), bcast(sdec)[:, None, :]


def _layer(x, pos_tile, mk16, mv16, s0, weights, *, ns, ts, lb, gblk, out_gvn):
    nb, rows, _ = x.shape
    m = ns * ts
    nt = rows // m
    has_state = s0 is not None
    g_norm, w_in16, g_ret, g_gmlp, w_s, b_s, w_out16, g_final = weights
    cos, sin = _rope_tables(pos_tile)
    dmask, qdec, kdec, sdec = _decay_tables(lb)
    ws_blk = w_s[:, :gblk, :gblk]
    bs_blk = jnp.broadcast_to(b_s[:, :gblk, None], (GMLP_GROUPS, gblk, HEAD_DIM))

    tile = lambda width: pl.BlockSpec((1, m, width), lambda b, t: (b, t, 0))
    per_group = lambda shape: pl.BlockSpec(shape, lambda b, t: (b,) + (0,) * (len(shape) - 1))
    state_shape = (ns, RET_HEADS, HEAD_DIM, HEAD_DIM)

    in_specs = [tile(D_MODEL),
                pl.BlockSpec((m, HEAD_DIM), lambda b, t: (t, 0)),
                pl.BlockSpec((m, HEAD_DIM), lambda b, t: (t, 0)),
                per_group((ns, MEM_LEN, XA_WIDTH)), per_group((ns, MEM_LEN, XA_WIDTH))]
    args = [x, cos, sin, mk16, mv16]
    if has_state:
        in_specs.append(per_group(state_shape))
        args.append(s0)
    consts = [g_norm.reshape(1, -1), w_in16, g_ret.reshape(1, -1), g_gmlp.reshape(1, -1), ws_blk, bs_blk,
              w_out16, g_final.reshape(1, -1), dmask, qdec, kdec, sdec]
    in_specs += [_const_spec(c.shape) for c in consts]
    args += consts

    out_shape = [jax.ShapeDtypeStruct(x.shape, _F32),
                 jax.ShapeDtypeStruct((nb * ns, RET_HEADS, HEAD_DIM, HEAD_DIM), _F32)]
    out_specs = [tile(D_MODEL), per_group(state_shape)]
    if out_gvn:
        out_shape.append(jax.ShapeDtypeStruct((nb, rows, GMLP_WIDTH), _F32))
        out_specs.append(tile(GMLP_WIDTH))

    return pl.pallas_call(
        functools.partial(_layer_kernel, ns=ns, ts=ts, lb=lb, gblk=gblk, has_state=has_state, out_gvn=out_gvn),
        grid=(nb, nt),
        in_specs=in_specs,
        out_specs=out_specs,
        out_shape=out_shape,
        scratch_shapes=[pltpu.VMEM((m, D_MODEL), _BF16), pltpu.VMEM((m, MIX_WIDTH), _BF16)],
        compiler_params=pltpu.CompilerParams(
            dimension_semantics=("arbitrary", "arbitrary"),
            vmem_limit_bytes=_V7X_VMEM_LIMIT_BYTES),
        name="layer_state" if has_state else "layer_prompt",
    )(*args)


def _memory_kv(mem, g_mem, w_mem_kv16):
    b = mem.shape[0]
    blk = lambda width: pl.BlockSpec((1, MEM_LEN, width), lambda i: (i, 0, 0))
    kv_f32 = jax.ShapeDtypeStruct((b, MEM_LEN, XA_WIDTH), _F32)
    kv_b16 = jax.ShapeDtypeStruct((b, MEM_LEN, XA_WIDTH), _BF16)
    return pl.pallas_call(
        _memkv_kernel,
        grid=(b,),
        in_specs=[blk(D_MODEL), _const_spec((1, D_MODEL)), _const_spec(w_mem_kv16.shape)],
        out_specs=[blk(XA_WIDTH)] * 4,
        out_shape=[kv_f32, kv_f32, kv_b16, kv_b16],
        compiler_params=pltpu.CompilerParams(dimension_semantics=("arbitrary",)),
        name="memory_kv",
    )(mem, g_mem.reshape(1, -1), w_mem_kv16)


PROMPT_TILE = 256
SAMPLE_GROUP = 4


def kernel(x_prompt, x_sample, mem_prompt, state_ret, cache_mem_k, cache_mem_v, g_norm, w_in, g_ret, g_gmlp,
           w_s, b_s, g_mem, w_mem_kv, w_out, g_final):
    depth = g_norm.shape[0]
    assert depth == 1
    b_p, l_p, _ = x_prompt.shape
    b_s_, l_s, _ = x_sample.shape
    weights = (g_norm[0], w_in[0].astype(_BF16), g_ret[0], g_gmlp[0], w_s[0], b_s[0],
               w_out[0].astype(_BF16), g_final)

    mk, mv, mk16, mv16 = _memory_kv(mem_prompt, g_mem[0], w_mem_kv[0].astype(_BF16))

    y_p, s_p = _layer(x_prompt, jnp.arange(l_p, dtype=jnp.int32), mk16, mv16, None, weights,
                      ns=1, ts=PROMPT_TILE, lb=PROMPT_TILE, gblk=GMLP_CHUNK, out_gvn=False)

    ns = SAMPLE_GROUP
    pos_s = jnp.tile(PAST_LEN + jnp.arange(l_s, dtype=jnp.int32), ns)
    ck16 = cache_mem_k[0].reshape(b_s_, MEM_LEN, XA_WIDTH).astype(_BF16)
    cv16 = cache_mem_v[0].reshape(b_s_, MEM_LEN, XA_WIDTH).astype(_BF16)
    y_s, s_s, gvn_s = _layer(x_sample.reshape(b_s_ // ns, ns * l_s, D_MODEL), pos_s, ck16, cv16, state_ret[0],
                             weights, ns=ns, ts=l_s, lb=l_s, gblk=l_s, out_gvn=True)

    kv_shape = (1, b_p, MEM_LEN, XA_HEADS, HEAD_DIM)
    return (y_p, y_s.reshape(x_sample.shape), s_p[None], mk.reshape(kv_shape), mv.reshape(kv_shape),
            s_s[None], gvn_s.reshape(1, b_s_, l_s, GMLP_WIDTH))
```

```python
import functools

import numpy as np

import jax
import jax.numpy as jnp
from jax import lax
from jax.experimental import pallas as pl
from jax.experimental.pallas import tpu as pltpu

D_MODEL = 2048
PAST_LEN = 1024
MEM_LEN = 256
RET_HEADS = 8
HEAD_DIM = 128
RET_WIDTH = RET_HEADS * HEAD_DIM
GMLP_GROUPS = 4
GMLP_WIDTH = GMLP_GROUPS * HEAD_DIM
GMLP_CHUNK = 128
XA_HEADS = 4
XA_WIDTH = XA_HEADS * HEAD_DIM
MIX_WIDTH = RET_WIDTH + GMLP_WIDTH + XA_WIDTH
ROPE_BASE = 10000.0
EPS = 1e-6

_RQ, _RK, _RV, _RG = 0, RET_WIDTH, 2 * RET_WIDTH, 3 * RET_WIDTH
_GU = 4 * RET_WIDTH
_GV = _GU + GMLP_WIDTH
_GG = _GV + GMLP_WIDTH
_AQ = _GG + GMLP_WIDTH
_AG = _AQ + XA_WIDTH

_V7X_VMEM_LIMIT_BYTES = 60 * 1024 * 1024

_BF16 = jnp.bfloat16
_F32 = jnp.float32


def _dot(a, b):
    return jnp.dot(a, b, preferred_element_type=_F32)


def _dot_nt(a, b):
    return lax.dot_general(a, b, (((1,), (1,)), ((), ())), preferred_element_type=_F32)


def _dot_tn(a, b):
    return lax.dot_general(a, b, (((0,), (0,)), ((), ())), preferred_element_type=_F32)


def _silu(x):
    return x / (1.0 + jnp.exp(-x))


def _rms_scale(x):
    return x * lax.rsqrt(jnp.mean(x * x, axis=-1, keepdims=True) + EPS)


def _center_scale(x):
    mu = jnp.mean(x, axis=-1, keepdims=True)
    d = x - mu
    return d * lax.rsqrt(jnp.mean(d * d, axis=-1, keepdims=True) + EPS)


def _memkv_kernel(mem_ref, g_ref, w_ref, k_ref, v_ref, k16_ref, v16_ref):
    n = (_rms_scale(mem_ref[0]) * g_ref[...]).astype(_BF16)
    kv = _dot(n, w_ref[...])
    k = kv[:, :XA_WIDTH]
    v = kv[:, XA_WIDTH:]
    k_ref[0] = k
    v_ref[0] = v
    k16_ref[0] = k.astype(_BF16)
    v16_ref[0] = v.astype(_BF16)


def _layer_kernel(*refs, ns, ts, lb, gblk, has_state, out_gvn, fuse_out):
    it = iter(refs)
    x_ref, cos_ref, sin_ref, mk_ref, mv_ref = (next(it) for _ in range(5))
    s0_ref = next(it) if has_state else None
    (gnorm_ref, win_ref, gret_ref, ggm_ref, ws_ref, bs_ref,
     dmask_ref, qdec_ref, kdec_ref, sdec_ref) = (next(it) for _ in range(10))
    wout_ref, gfin_ref = (next(it), next(it)) if fuse_out else (None, None)
    y_ref, s_ref = next(it), next(it)
    gvn_ref = next(it) if out_gvn else None
    h_s = next(it)
    mix_s = next(it) if fuse_out else y_ref.at[0]

    @pl.when(pl.program_id(1) == 0)
    def _():
        if has_state:
            s_ref[...] = s0_ref[...]
        else:
            s_ref[...] = jnp.zeros_like(s_ref)

    h_s[...] = (_rms_scale(x_ref[0]) * gnorm_ref[...]).astype(_BF16)

    def proj(c0, width):
        return _dot(h_s[...], win_ref[:, c0:c0 + width])

    cos = cos_ref[...]
    sin = sin_ref[...]

    def rope(u):
        return u * cos + pltpu.roll(u, HEAD_DIM // 2, 1) * sin

    for p in range(RET_HEADS // 2):
        c0 = 2 * HEAD_DIM * p
        q2, k2, v2, g2 = (proj(sec + c0, 2 * HEAD_DIM) for sec in (_RQ, _RK, _RV, _RG))
        for hh in range(2):
            head = 2 * p + hh
            hs = slice(hh * HEAD_DIM, (hh + 1) * HEAD_DIM)
            cs = slice(head * HEAD_DIM, (head + 1) * HEAD_DIM)
            qh = rope(q2[:, hs])
            kh = rope(k2[:, hs]) * (HEAD_DIM ** -0.5)
            vh = v2[:, hs].astype(_BF16)
            gh = g2[:, hs]
            for s in range(ns):
                for bi in range(ts // lb):
                    rs = slice(s * ts + bi * lb, s * ts + (bi + 1) * lb)
                    qb = qh[rs].astype(_BF16)
                    kb = kh[rs]
                    vb = vh[rs]
                    state = s_ref[s, head]
                    sc = _dot_nt(qb, kb.astype(_BF16)) * dmask_ref[head]
                    o = _dot(sc.astype(_BF16), vb) + _dot(qb, state.astype(_BF16)) * qdec_ref[head]
                    s_ref[s, head] = state * sdec_ref[head] + _dot_tn(
                        (kb * kdec_ref[head]).astype(_BF16), vb)
                    on = _center_scale(o) * gret_ref[:, cs]
                    mix_s[rs, cs] = (on * _silu(gh[rs])).astype(_BF16)

    gu = proj(_GU, GMLP_WIDTH)
    gvn = _center_scale(proj(_GV, GMLP_WIDTH)) * ggm_ref[...]
    gg = proj(_GG, GMLP_WIDTH)
    if out_gvn:
        gvn_ref[0] = gvn
    row = lax.broadcasted_iota(jnp.int32, (gblk, gblk), 0)
    col = lax.broadcasted_iota(jnp.int32, (gblk, gblk), 1)
    for g in range(GMLP_GROUPS):
        hs = slice(g * HEAD_DIM, (g + 1) * HEAD_DIM)
        cs = slice(RET_WIDTH + g * HEAD_DIM, RET_WIDTH + (g + 1) * HEAD_DIM)
        wm = jnp.where(row >= col, ws_ref[g], 0.0).astype(_BF16)
        for c in range(ns * ts // gblk):
            rs = slice(c * gblk, (c + 1) * gblk)
            sg = _dot(wm, gvn[rs, hs].astype(_BF16)) + bs_ref[g]
            mix_s[rs, cs] = (gu[rs, hs] * sg * _silu(gg[rs, hs])).astype(_BF16)

    aq = proj(_AQ, XA_WIDTH)
    ag = proj(_AG, XA_WIDTH)
    for s in range(ns):
        rs = slice(s * ts, (s + 1) * ts)
        for hd in range(XA_HEADS):
            hs = slice(hd * HEAD_DIM, (hd + 1) * HEAD_DIM)
            cs = slice(RET_WIDTH + GMLP_WIDTH + hd * HEAD_DIM,
                       RET_WIDTH + GMLP_WIDTH + (hd + 1) * HEAD_DIM)
            sc = _dot_nt(aq[rs, hs].astype(_BF16), mk_ref[s, :, hs]) * (HEAD_DIM ** -0.5)
            e = jnp.exp(sc - jnp.max(sc, axis=-1, keepdims=True))
            ao = _dot(e.astype(_BF16), mv_ref[s, :, hs]) / jnp.sum(e, axis=-1, keepdims=True)
            mix_s[rs, cs] = (ao * _silu(ag[rs, hs])).astype(_BF16)

    if fuse_out:
        y = x_ref[0] + _dot(mix_s[...], wout_ref[...])
        y_ref[0] = _rms_scale(y) * gfin_ref[...]


def _outproj_kernel(x_ref, mix_ref, w_ref, g_ref, y_ref):
    y = x_ref[...] + _dot(mix_ref[...], w_ref[...])
    y_ref[...] = _rms_scale(y) * g_ref[...]


def _const_spec(shape):
    return pl.BlockSpec(shape, lambda *_: (0,) * len(shape), pipeline_mode=pl.Buffered(1))


def _rope_tables(pos):
    inv_freq = ROPE_BASE ** (-np.arange(0, HEAD_DIM, 2, dtype=np.float64) / HEAD_DIM)
    ang = np.asarray(pos, np.float64)[:, None] * inv_freq[None, :]
    cos, sin = np.cos(ang), np.sin(ang)
    return (np.concatenate([cos, cos], axis=-1).astype(np.float32),
            np.concatenate([-sin, sin], axis=-1).astype(np.float32))


def _decay_tables(lb):
    log_gamma = np.log(1.0 - 2.0 ** (-5.0 - np.arange(RET_HEADS, dtype=np.float64)))
    idx = np.arange(lb, dtype=np.float64)
    diff = idx[:, None] - idx[None, :]
    dmask = np.where(diff >= 0, np.exp(log_gamma[:, None, None] * np.maximum(diff, 0.0)[None]), 0.0)
    qdec = np.exp(log_gamma[:, None] * (idx[None, :] + 1.0))
    kdec = np.exp(log_gamma[:, None] * (lb - 1.0 - idx[None, :]))
    sdec = np.exp(log_gamma * lb)
    bcast = lambda a: np.broadcast_to(a[..., None], a.shape + (HEAD_DIM,)).astype(np.float32)
    return dmask.astype(np.float32), bcast(qdec), bcast(kdec), bcast(sdec)[:, None, :]


def _layer(x, pos_tile, mk16, mv16, s0, weights, *, ns, ts, lb, gblk, out_gvn, fuse_out):
    nb, rows, _ = x.shape
    m = ns * ts
    nt = rows // m
    has_state = s0 is not None
    g_norm, w_in16, g_ret, g_gmlp, w_s, b_s, w_out16, g_final = weights
    cos, sin = _rope_tables(pos_tile)
    dmask, qdec, kdec, sdec = _decay_tables(lb)
    ws_blk = w_s[:, :gblk, :gblk]
    bs_blk = jnp.broadcast_to(b_s[:, :gblk, None], (GMLP_GROUPS, gblk, HEAD_DIM))

    tile = lambda width: pl.BlockSpec((1, m, width), lambda b, t: (b, t, 0))
    per_group = lambda shape: pl.BlockSpec(shape, lambda b, t: (b,) + (0,) * (len(shape) - 1))
    state_shape = (ns, RET_HEADS, HEAD_DIM, HEAD_DIM)

    in_specs = [tile(D_MODEL),
                pl.BlockSpec((m, HEAD_DIM), lambda b, t: (t, 0)),
                pl.BlockSpec((m, HEAD_DIM), lambda b, t: (t, 0)),
                per_group((ns, MEM_LEN, XA_WIDTH)), per_group((ns, MEM_LEN, XA_WIDTH))]
    args = [x, cos, sin, mk16, mv16]
    if has_state:
        in_specs.append(per_group(state_shape))
        args.append(s0)
    consts = [g_norm.reshape(1, -1), w_in16, g_ret.reshape(1, -1), g_gmlp.reshape(1, -1), ws_blk, bs_blk,
              dmask, qdec, kdec, sdec]
    if fuse_out:
        consts += [w_out16, g_final.reshape(1, -1)]
    in_specs += [_const_spec(c.shape) for c in consts]
    args += consts

    out_shape = [jax.ShapeDtypeStruct(x.shape, _F32 if fuse_out else _BF16),
                 jax.ShapeDtypeStruct((nb * ns, RET_HEADS, HEAD_DIM, HEAD_DIM), _F32)]
    out_specs = [tile(D_MODEL), per_group(state_shape)]
    if out_gvn:
        out_shape.append(jax.ShapeDtypeStruct((nb, rows, GMLP_WIDTH), _F32))
        out_specs.append(tile(GMLP_WIDTH))
    scratch = [pltpu.VMEM((m, D_MODEL), _BF16)]
    if fuse_out:
        scratch.append(pltpu.VMEM((m, MIX_WIDTH), _BF16))

    return pl.pallas_call(
        functools.partial(_layer_kernel, ns=ns, ts=ts, lb=lb, gblk=gblk, has_state=has_state, out_gvn=out_gvn,
                          fuse_out=fuse_out),
        grid=(nb, nt),
        in_specs=in_specs,
        out_specs=out_specs,
        out_shape=out_shape,
        scratch_shapes=scratch,
        compiler_params=pltpu.CompilerParams(
            dimension_semantics=("arbitrary", "arbitrary"),
            vmem_limit_bytes=_V7X_VMEM_LIMIT_BYTES),
        name="layer_state" if has_state else "layer_prompt",
    )(*args)


def _out_proj(x, mix, w_out16, g_final, *, tile_rows):
    rows = x.shape[0]
    blk = pl.BlockSpec((tile_rows, D_MODEL), lambda i: (i, 0))
    return pl.pallas_call(
        _outproj_kernel,
        grid=(rows // tile_rows,),
        in_specs=[blk, blk, _const_spec(w_out16.shape), _const_spec((1, D_MODEL))],
        out_specs=blk,
        out_shape=jax.ShapeDtypeStruct(x.shape, _F32),
        compiler_params=pltpu.CompilerParams(
            dimension_semantics=("arbitrary",), vmem_limit_bytes=_V7X_VMEM_LIMIT_BYTES),
        name="out_proj",
    )(x, mix, w_out16, g_final.reshape(1, -1))


def _memory_kv(mem, g_mem, w_mem_kv16):
    b = mem.shape[0]
    blk = lambda width: pl.BlockSpec((1, MEM_LEN, width), lambda i: (i, 0, 0))
    kv_f32 = jax.ShapeDtypeStruct((b, MEM_LEN, XA_WIDTH), _F32)
    kv_b16 = jax.ShapeDtypeStruct((b, MEM_LEN, XA_WIDTH), _BF16)
    return pl.pallas_call(
        _memkv_kernel,
        grid=(b,),
        in_specs=[blk(D_MODEL), _const_spec((1, D_MODEL)), _const_spec(w_mem_kv16.shape)],
        out_specs=[blk(XA_WIDTH)] * 4,
        out_shape=[kv_f32, kv_f32, kv_b16, kv_b16],
        compiler_params=pltpu.CompilerParams(dimension_semantics=("arbitrary",)),
        name="memory_kv",
    )(mem, g_mem.reshape(1, -1), w_mem_kv16)


PROMPT_TILE = 512
RET_BLOCK = 256
OUT_TILE = 512
SAMPLE_GROUP = 4


def kernel(x_prompt, x_sample, mem_prompt, state_ret, cache_mem_k, cache_mem_v, g_norm, w_in, g_ret, g_gmlp,
           w_s, b_s, g_mem, w_mem_kv, w_out, g_final):
    depth = g_norm.shape[0]
    assert depth == 1
    b_p, l_p, _ = x_prompt.shape
    b_s_, l_s, _ = x_sample.shape
    w_out16 = w_out[0].astype(_BF16)
    weights = (g_norm[0], w_in[0].astype(_BF16), g_ret[0], g_gmlp[0], w_s[0], b_s[0], w_out16, g_final)

    mk, mv, mk16, mv16 = _memory_kv(mem_prompt, g_mem[0], w_mem_kv[0].astype(_BF16))

    mix_p, s_p = _layer(x_prompt, np.arange(l_p), mk16, mv16, None, weights,
                        ns=1, ts=PROMPT_TILE, lb=RET_BLOCK, gblk=GMLP_CHUNK, out_gvn=False, fuse_out=False)
    y_p = _out_proj(x_prompt.reshape(b_p * l_p, D_MODEL), mix_p.reshape(b_p * l_p, MIX_WIDTH), w_out16, g_final,
                    tile_rows=OUT_TILE).reshape(x_prompt.shape)

    ns = SAMPLE_GROUP
    pos_s = np.tile(PAST_LEN + np.arange(l_s), ns)
    ck16 = cache_mem_k[0].reshape(b_s_, MEM_LEN, XA_WIDTH).astype(_BF16)
    cv16 = cache_mem_v[0].reshape(b_s_, MEM_LEN, XA_WIDTH).astype(_BF16)
    y_s, s_s, gvn_s = _layer(x_sample.reshape(b_s_ // ns, ns * l_s, D_MODEL), pos_s, ck16, cv16, state_ret[0],
                             weights, ns=ns, ts=l_s, lb=l_s, gblk=l_s, out_gvn=True, fuse_out=True)

    kv_shape = (1, b_p, MEM_LEN, XA_HEADS, HEAD_DIM)
    return (y_p, y_s.reshape(x_sample.shape), s_p[None], mk.reshape(kv_shape), mv.reshape(kv_shape),
            s_s[None], gvn_s.reshape(1, b_s_, l_s, GMLP_WIDTH))
```

```python
import functools

import numpy as np

import jax
import jax.numpy as jnp
from jax import lax
from jax.experimental import pallas as pl
from jax.experimental.pallas import tpu as pltpu

D_MODEL = 2048
PAST_LEN = 1024
MEM_LEN = 256
RET_HEADS = 8
HEAD_DIM = 128
RET_WIDTH = RET_HEADS * HEAD_DIM
GMLP_GROUPS = 4
GMLP_WIDTH = GMLP_GROUPS * HEAD_DIM
GMLP_CHUNK = 128
XA_HEADS = 4
XA_WIDTH = XA_HEADS * HEAD_DIM
ROPE_BASE = 10000.0
EPS = 1e-6

_RQ, _RK, _RV, _RG = 0, RET_WIDTH, 2 * RET_WIDTH, 3 * RET_WIDTH
_RET_SLAB = 4 * RET_WIDTH
_GU, _GV, _GG = 0, GMLP_WIDTH, 2 * GMLP_WIDTH
_AQ = 3 * GMLP_WIDTH
_AG = _AQ + XA_WIDTH

_V7X_VMEM_LIMIT_BYTES = 60 * 1024 * 1024

_BF16 = jnp.bfloat16
_F32 = jnp.float32


def _dot(a, b):
    return jnp.dot(a, b, preferred_element_type=_F32)


def _dot_nt(a, b):
    return lax.dot_general(a, b, (((1,), (1,)), ((), ())), preferred_element_type=_F32)


def _dot_tn(a, b):
    return lax.dot_general(a, b, (((0,), (0,)), ((), ())), preferred_element_type=_F32)


def _silu(x):
    return x / (1.0 + jnp.exp(-x))


def _rms_scale(x):
    return x * lax.rsqrt(jnp.mean(x * x, axis=-1, keepdims=True) + EPS)


def _center_scale(x):
    mu = jnp.mean(x, axis=-1, keepdims=True)
    d = x - mu
    return d * lax.rsqrt(jnp.mean(d * d, axis=-1, keepdims=True) + EPS)


def _memkv_kernel(mem_ref, g_ref, w_ref, k_ref, v_ref, k16_ref, v16_ref):
    n = (_rms_scale(mem_ref[0]) * g_ref[...]).astype(_BF16)
    kv = _dot(n, w_ref[...])
    k = kv[:, :XA_WIDTH]
    v = kv[:, XA_WIDTH:]
    k_ref[0] = k
    v_ref[0] = v
    k16_ref[0] = k.astype(_BF16)
    v16_ref[0] = v.astype(_BF16)


def _retention_kernel(*refs, ns, ts, lb, has_state):
    it = iter(refs)
    x_ref, cos_ref, sin_ref = (next(it) for _ in range(3))
    s0_ref = next(it) if has_state else None
    gnorm_ref, w_ref, gret_ref, dmask_ref, qdec_ref, kdec_ref, sdec_ref = (next(it) for _ in range(7))
    ro_ref, s_ref, h_s = next(it), next(it), next(it)

    @pl.when(pl.program_id(1) == 0)
    def _():
        if has_state:
            s_ref[...] = s0_ref[...]
        else:
            s_ref[...] = jnp.zeros_like(s_ref)

    h_s[...] = (_rms_scale(x_ref[0]) * gnorm_ref[...]).astype(_BF16)
    cos = cos_ref[...]
    sin = sin_ref[...]

    def rope(u):
        return u * cos + pltpu.roll(u, HEAD_DIM // 2, 1) * sin

    for p in range(RET_HEADS // 2):
        c0 = 2 * HEAD_DIM * p
        q2, k2, v2, g2 = (_dot(h_s[...], w_ref[:, sec + c0:sec + c0 + 2 * HEAD_DIM])
                          for sec in (_RQ, _RK, _RV, _RG))
        for hh in range(2):
            head = 2 * p + hh
            hs = slice(hh * HEAD_DIM, (hh + 1) * HEAD_DIM)
            cs = slice(head * HEAD_DIM, (head + 1) * HEAD_DIM)
            qh = rope(q2[:, hs])
            kh = rope(k2[:, hs]) * (HEAD_DIM ** -0.5)
            vh = v2[:, hs].astype(_BF16)
            gh = g2[:, hs]
            for s in range(ns):
                for bi in range(ts // lb):
                    rs = slice(s * ts + bi * lb, s * ts + (bi + 1) * lb)
                    qb = qh[rs].astype(_BF16)
                    kb = kh[rs]
                    vb = vh[rs]
                    state = s_ref[s, head]
                    sc = _dot_nt(qb, kb.astype(_BF16)) * dmask_ref[head]
                    o = _dot(sc.astype(_BF16), vb) + _dot(qb, state.astype(_BF16)) * qdec_ref[head]
                    s_ref[s, head] = state * sdec_ref[head] + _dot_tn(
                        (kb * kdec_ref[head]).astype(_BF16), vb)
                    on = _center_scale(o) * gret_ref[:, cs]
                    ro_ref[0, rs, cs] = (on * _silu(gh[rs])).astype(_BF16)


def _mix_out_kernel(*refs, ns, ts, gblk, out_gvn):
    it = iter(refs)
    (x_ref, ro_ref, mk_ref, mv_ref, gnorm_ref, w_ref, ggm_ref, ws_ref, bs_ref, wout_ref,
     gfin_ref) = (next(it) for _ in range(11))
    y_ref = next(it)
    gvn_ref = next(it) if out_gvn else None
    h_s, mix_s = next(it), next(it)

    x = x_ref[0]
    h_s[...] = (_rms_scale(x) * gnorm_ref[...]).astype(_BF16)

    def proj(c0, width):
        return _dot(h_s[...], w_ref[:, c0:c0 + width])

    gu = proj(_GU, GMLP_WIDTH)
    gvn = _center_scale(proj(_GV, GMLP_WIDTH)) * ggm_ref[...]
    gg = proj(_GG, GMLP_WIDTH)
    if out_gvn:
        gvn_ref[0] = gvn
    row = lax.broadcasted_iota(jnp.int32, (gblk, gblk), 0)
    col = lax.broadcasted_iota(jnp.int32, (gblk, gblk), 1)
    for g in range(GMLP_GROUPS):
        hs = slice(g * HEAD_DIM, (g + 1) * HEAD_DIM)
        wm = jnp.where(row >= col, ws_ref[g], 0.0).astype(_BF16)
        for c in range(ns * ts // gblk):
            rs = slice(c * gblk, (c + 1) * gblk)
            sg = _dot(wm, gvn[rs, hs].astype(_BF16)) + bs_ref[g]
            mix_s[rs, hs] = (gu[rs, hs] * sg * _silu(gg[rs, hs])).astype(_BF16)

    aq = proj(_AQ, XA_WIDTH)
    ag = proj(_AG, XA_WIDTH)
    for s in range(ns):
        rs = slice(s * ts, (s + 1) * ts)
        for hd in range(XA_HEADS):
            hs = slice(hd * HEAD_DIM, (hd + 1) * HEAD_DIM)
            cs = slice(GMLP_WIDTH + hd * HEAD_DIM, GMLP_WIDTH + (hd + 1) * HEAD_DIM)
            sc = _dot_nt(aq[rs, hs].astype(_BF16), mk_ref[s, :, hs]) * (HEAD_DIM ** -0.5)
            e = jnp.exp(sc - jnp.max(sc, axis=-1, keepdims=True))
            ao = _dot(e.astype(_BF16), mv_ref[s, :, hs]) / jnp.sum(e, axis=-1, keepdims=True)
            mix_s[rs, cs] = (ao * _silu(ag[rs, hs])).astype(_BF16)

    y = x + _dot(ro_ref[0], wout_ref[:RET_WIDTH, :]) + _dot(mix_s[...], wout_ref[RET_WIDTH:, :])
    y_ref[0] = _rms_scale(y) * gfin_ref[...]


def _const_spec(shape):
    return pl.BlockSpec(shape, lambda *_: (0,) * len(shape), pipeline_mode=pl.Buffered(1))


def _rope_tables(pos):
    inv_freq = ROPE_BASE ** (-np.arange(0, HEAD_DIM, 2, dtype=np.float64) / HEAD_DIM)
    ang = np.asarray(pos, np.float64)[:, None] * inv_freq[None, :]
    cos, sin = np.cos(ang), np.sin(ang)
    return (np.concatenate([cos, cos], axis=-1).astype(np.float32),
            np.concatenate([-sin, sin], axis=-1).astype(np.float32))


def _decay_tables(lb):
    log_gamma = np.log(1.0 - 2.0 ** (-5.0 - np.arange(RET_HEADS, dtype=np.float64)))
    idx = np.arange(lb, dtype=np.float64)
    diff = idx[:, None] - idx[None, :]
    dmask = np.where(diff >= 0, np.exp(log_gamma[:, None, None] * np.maximum(diff, 0.0)[None]), 0.0)
    qdec = np.exp(log_gamma[:, None] * (idx[None, :] + 1.0))
    kdec = np.exp(log_gamma[:, None] * (lb - 1.0 - idx[None, :]))
    sdec = np.exp(log_gamma * lb)
    bcast = lambda a: np.broadcast_to(a[..., None], a.shape + (HEAD_DIM,)).astype(np.float32)
    return dmask.astype(np.float32), bcast(qdec), bcast(kdec), bcast(sdec)[:, None, :]


def _tile_specs(m):
    tile = lambda width: pl.BlockSpec((1, m, width), lambda b, t: (b, t, 0))
    per_group = lambda shape: pl.BlockSpec(shape, lambda b, t: (b,) + (0,) * (len(shape) - 1))
    return tile, per_group


def _retention(x, pos_tile, s0, g_norm, w_ret16, g_ret, *, ns, ts, lb, name):
    nb, rows, _ = x.shape
    m = ns * ts
    has_state = s0 is not None
    tile, per_group = _tile_specs(m)
    state_shape = (ns, RET_HEADS, HEAD_DIM, HEAD_DIM)
    rope_spec = pl.BlockSpec((m, HEAD_DIM), lambda b, t: (t, 0))

    in_specs = [tile(D_MODEL), rope_spec, rope_spec]
    args = [x, *_rope_tables(pos_tile)]
    if has_state:
        in_specs.append(per_group(state_shape))
        args.append(s0)
    consts = [g_norm.reshape(1, -1), w_ret16, g_ret.reshape(1, -1), *_decay_tables(lb)]
    in_specs += [_const_spec(c.shape) for c in consts]
    args += consts

    return pl.pallas_call(
        functools.partial(_retention_kernel, ns=ns, ts=ts, lb=lb, has_state=has_state),
        grid=(nb, rows // m),
        in_specs=in_specs,
        out_specs=[tile(RET_WIDTH), per_group(state_shape)],
        out_shape=[jax.ShapeDtypeStruct((nb, rows, RET_WIDTH), _BF16),
                   jax.ShapeDtypeStruct((nb * ns, RET_HEADS, HEAD_DIM, HEAD_DIM), _F32)],
        scratch_shapes=[pltpu.VMEM((m, D_MODEL), _BF16)],
        compiler_params=pltpu.CompilerParams(
            dimension_semantics=("arbitrary", "arbitrary"), vmem_limit_bytes=_V7X_VMEM_LIMIT_BYTES),
        name=name,
    )(*args)


def _mix_out(x, ro, mk16, mv16, g_norm, w_rest16, g_gmlp, w_s, b_s, w_out16, g_final, *, ns, ts, gblk, out_gvn,
             name):
    nb, rows, _ = x.shape
    m = ns * ts
    tile, per_group = _tile_specs(m)
    ws_blk = w_s[:, :gblk, :gblk]
    bs_blk = jnp.broadcast_to(b_s[:, :gblk, None], (GMLP_GROUPS, gblk, HEAD_DIM))
    kv_spec = per_group((ns, MEM_LEN, XA_WIDTH))
    consts = [g_norm.reshape(1, -1), w_rest16, g_gmlp.reshape(1, -1), ws_blk, bs_blk, w_out16,
              g_final.reshape(1, -1)]

    out_shape = [jax.ShapeDtypeStruct(x.shape, _F32)]
    out_specs = [tile(D_MODEL)]
    if out_gvn:
        out_shape.append(jax.ShapeDtypeStruct((nb, rows, GMLP_WIDTH), _F32))
        out_specs.append(tile(GMLP_WIDTH))

    return pl.pallas_call(
        functools.partial(_mix_out_kernel, ns=ns, ts=ts, gblk=gblk, out_gvn=out_gvn),
        grid=(nb, rows // m),
        in_specs=[tile(D_MODEL), tile(RET_WIDTH), kv_spec, kv_spec] + [_const_spec(c.shape) for c in consts],
        out_specs=out_specs,
        out_shape=out_shape,
        scratch_shapes=[pltpu.VMEM((m, D_MODEL), _BF16), pltpu.VMEM((m, GMLP_WIDTH + XA_WIDTH), _BF16)],
        compiler_params=pltpu.CompilerParams(
            dimension_semantics=("arbitrary", "arbitrary"), vmem_limit_bytes=_V7X_VMEM_LIMIT_BYTES),
        name=name,
    )(x, ro, mk16, mv16, *consts)


def _memory_kv(mem, g_mem, w_mem_kv16):
    b = mem.shape[0]
    blk = lambda width: pl.BlockSpec((1, MEM_LEN, width), lambda i: (i, 0, 0))
    kv_f32 = jax.ShapeDtypeStruct((b, MEM_LEN, XA_WIDTH), _F32)
    kv_b16 = jax.ShapeDtypeStruct((b, MEM_LEN, XA_WIDTH), _BF16)
    return pl.pallas_call(
        _memkv_kernel,
        grid=(b,),
        in_specs=[blk(D_MODEL), _const_spec((1, D_MODEL)), _const_spec(w_mem_kv16.shape)],
        out_specs=[blk(XA_WIDTH)] * 4,
        out_shape=[kv_f32, kv_f32, kv_b16, kv_b16],
        compiler_params=pltpu.CompilerParams(dimension_semantics=("arbitrary",)),
        name="memory_kv",
    )(mem, g_mem.reshape(1, -1), w_mem_kv16)


PROMPT_TILE = 512
RET_BLOCK = 256
SAMPLE_GROUP = 8


def kernel(x_prompt, x_sample, mem_prompt, state_ret, cache_mem_k, cache_mem_v, g_norm, w_in, g_ret, g_gmlp,
           w_s, b_s, g_mem, w_mem_kv, w_out, g_final):
    assert g_norm.shape[0] == 1
    b_p, l_p, _ = x_prompt.shape
    b_s_, l_s, _ = x_sample.shape
    w_ret16 = w_in[0, :, :_RET_SLAB].astype(_BF16)
    w_rest16 = w_in[0, :, _RET_SLAB:].astype(_BF16)
    w_out16 = w_out[0].astype(_BF16)
    ret_w = (g_norm[0], w_ret16, g_ret[0])
    mix_w = (g_norm[0], w_rest16, g_gmlp[0], w_s[0], b_s[0], w_out16, g_final)

    mk, mv, mk16, mv16 = _memory_kv(mem_prompt, g_mem[0], w_mem_kv[0].astype(_BF16))

    ro_p, s_p = _retention(x_prompt, np.arange(l_p), None, *ret_w,
                           ns=1, ts=PROMPT_TILE, lb=RET_BLOCK, name="retention_prompt")
    (y_p,) = _mix_out(x_prompt, ro_p, mk16, mv16, *mix_w,
                      ns=1, ts=PROMPT_TILE, gblk=GMLP_CHUNK, out_gvn=False, name="mix_out_prompt")

    ns = SAMPLE_GROUP
    xs = x_sample.reshape(b_s_ // ns, ns * l_s, D_MODEL)
    ck16 = cache_mem_k[0].reshape(b_s_, MEM_LEN, XA_WIDTH).astype(_BF16)
    cv16 = cache_mem_v[0].reshape(b_s_, MEM_LEN, XA_WIDTH).astype(_BF16)
    ro_s, s_s = _retention(xs, np.tile(PAST_LEN + np.arange(l_s), ns), state_ret[0], *ret_w,
                           ns=ns, ts=l_s, lb=l_s, name="retention_sample")
    y_s, gvn_s = _mix_out(xs, ro_s, ck16, cv16, *mix_w,
                          ns=ns, ts=l_s, gblk=l_s, out_gvn=True, name="mix_out_sample")

    kv_shape = (1, b_p, MEM_LEN, XA_HEADS, HEAD_DIM)
    return (y_p, y_s.reshape(x_sample.shape), s_p[None], mk.reshape(kv_shape), mv.reshape(kv_shape),
            s_s[None], gvn_s.reshape(1, b_s_, l_s, GMLP_WIDTH))
```

```python
import functools

import numpy as np

import jax
import jax.numpy as jnp
from jax import lax
from jax.experimental import pallas as pl
from jax.experimental.pallas import tpu as pltpu

D_MODEL = 2048
PAST_LEN = 1024
MEM_LEN = 256
RET_HEADS = 8
HEAD_DIM = 128
RET_WIDTH = RET_HEADS * HEAD_DIM
GMLP_GROUPS = 4
GMLP_WIDTH = GMLP_GROUPS * HEAD_DIM
GMLP_CHUNK = 128
XA_HEADS = 4
XA_WIDTH = XA_HEADS * HEAD_DIM
MIX_WIDTH = RET_WIDTH + GMLP_WIDTH + XA_WIDTH
ROPE_BASE = 10000.0
EPS = 1e-6

_RQ, _RK, _RV, _RG = 0, RET_WIDTH, 2 * RET_WIDTH, 3 * RET_WIDTH
_GU = 4 * RET_WIDTH
_GV = _GU + GMLP_WIDTH
_GG = _GV + GMLP_WIDTH
_AQ = _GG + GMLP_WIDTH
_AG = _AQ + XA_WIDTH
IN_WIDTH = _AG + XA_WIDTH
_MIX_GMLP = RET_WIDTH
_MIX_XA = RET_WIDTH + GMLP_WIDTH

_V7X_VMEM_LIMIT_BYTES = 60 * 1024 * 1024

_BF16 = jnp.bfloat16
_F32 = jnp.float32


def _dot(a, b):
    return jnp.dot(a, b, preferred_element_type=_F32)


def _dot_nt(a, b):
    return lax.dot_general(a, b, (((1,), (1,)), ((), ())), preferred_element_type=_F32)


def _dot_tn(a, b):
    return lax.dot_general(a, b, (((0,), (0,)), ((), ())), preferred_element_type=_F32)


def _silu(x):
    return x / (1.0 + jnp.exp(-x))


def _rms_scale(x):
    return x * lax.rsqrt(jnp.mean(x * x, axis=-1, keepdims=True) + EPS)


def _center_scale(x):
    mu = jnp.mean(x, axis=-1, keepdims=True)
    d = x - mu
    return d * lax.rsqrt(jnp.mean(d * d, axis=-1, keepdims=True) + EPS)


def _rope(u, cos2, sin2):
    return u * cos2 + pltpu.roll(u, HEAD_DIM // 2, 1) * sin2


def _cols(base, i):
    return slice(base + i * HEAD_DIM, base + (i + 1) * HEAD_DIM)


def _retention_blocks(blocks, dmask_ref, qdec_ref, kdec_ref, sdec_ref):
    sc = [_dot_nt(qb, kb.astype(_BF16)) for qb, kb, _, _, _ in blocks]
    cross = [_dot(qb, state.astype(_BF16)) for qb, _, _, state, _ in blocks]
    new_states = [state * sdec_ref[head] + _dot_tn((kb * kdec_ref[head]).astype(_BF16), vb)
                  for _, kb, vb, state, head in blocks]
    o = [_dot((s * dmask_ref[head]).astype(_BF16), vb) + c * qdec_ref[head]
         for s, c, (_, _, vb, _, head) in zip(sc, cross, blocks)]
    return o, new_states


def _tril_bf16(w):
    n = w.shape[0]
    row = lax.broadcasted_iota(jnp.int32, (n, n), 0)
    col = lax.broadcasted_iota(jnp.int32, (n, n), 1)
    return jnp.where(row >= col, w, 0.0).astype(_BF16)


def _xattn_heads(heads):
    sc = [_dot_nt(aq.astype(_BF16), mk) * (HEAD_DIM ** -0.5) for aq, mk, _ in heads]
    e = [jnp.exp(s - jnp.max(s, axis=-1, keepdims=True)) for s in sc]
    return [_dot(p.astype(_BF16), mv) / jnp.sum(p, axis=-1, keepdims=True) for p, (_, _, mv) in zip(e, heads)]


def _memkv_kernel(mem_ref, g_ref, w_ref, k_ref, v_ref, k16_ref, v16_ref):
    n = (_rms_scale(mem_ref[0]) * g_ref[...]).astype(_BF16)
    kv = _dot(n, w_ref[...])
    k = kv[:, :XA_WIDTH]
    v = kv[:, XA_WIDTH:]
    for hd in range(XA_HEADS):
        k_ref[0, :, hd, :] = k[:, _cols(0, hd)]
        v_ref[0, :, hd, :] = v[:, _cols(0, hd)]
    k16_ref[0] = k.astype(_BF16)
    v16_ref[0] = v.astype(_BF16)


def _prompt_kernel(x_ref, cos_ref, sin_ref, mk_ref, mv_ref, gnorm_ref, win_ref, gret_ref, ggm_ref, ws_ref, bs_ref,
                   dmask_ref, qdec_ref, kdec_ref, sdec_ref, mix_ref, s_ref, h_s, *, lb):
    ts = x_ref.shape[1]

    @pl.when(pl.program_id(1) == 0)
    def _():
        s_ref[...] = jnp.zeros_like(s_ref)

    h_s[...] = (_rms_scale(x_ref[0]) * gnorm_ref[...]).astype(_BF16)

    def proj(c0, width):
        return _dot(h_s[...], win_ref[:, c0:c0 + width])

    cos = cos_ref[...]
    sin = sin_ref[...]

    decay = (dmask_ref, qdec_ref, kdec_ref, sdec_ref)
    for p in range(RET_HEADS // 2):
        c0 = 2 * HEAD_DIM * p
        q2, k2, v2, g2 = (proj(sec + c0, 2 * HEAD_DIM) for sec in (_RQ, _RK, _RV, _RG))
        heads = (2 * p, 2 * p + 1)
        qh = [_rope(q2[:, _cols(0, hh)], cos, sin).astype(_BF16) for hh in range(2)]
        kh = [_rope(k2[:, _cols(0, hh)], cos, sin) * (HEAD_DIM ** -0.5) for hh in range(2)]
        vh = [v2[:, _cols(0, hh)].astype(_BF16) for hh in range(2)]
        for bi in range(ts // lb):
            rs = slice(bi * lb, (bi + 1) * lb)
            o, new_states = _retention_blocks(
                [(qh[hh][rs], kh[hh][rs], vh[hh][rs], s_ref[0, head], head) for hh, head in enumerate(heads)],
                *decay)
            for hh, head in enumerate(heads):
                s_ref[0, head] = new_states[hh]
                on = _center_scale(o[hh]) * gret_ref[:, _cols(0, head)]
                mix_ref[0, rs, _cols(0, head)] = (on * _silu(g2[rs, _cols(0, hh)])).astype(_BF16)

    gu = proj(_GU, GMLP_WIDTH)
    gvn = _center_scale(proj(_GV, GMLP_WIDTH)) * ggm_ref[...]
    gg = proj(_GG, GMLP_WIDTH)
    for g in range(GMLP_GROUPS):
        hs = _cols(0, g)
        wm = _tril_bf16(ws_ref[g])
        for c in range(ts // GMLP_CHUNK):
            rs = slice(c * GMLP_CHUNK, (c + 1) * GMLP_CHUNK)
            sg = _dot(wm, gvn[rs, hs].astype(_BF16)) + bs_ref[g]
            mix_ref[0, rs, _cols(_MIX_GMLP, g)] = (gu[rs, hs] * sg * _silu(gg[rs, hs])).astype(_BF16)

    aq = proj(_AQ, XA_WIDTH)
    ag = proj(_AG, XA_WIDTH)
    ao = _xattn_heads([(aq[:, _cols(0, hd)], mk_ref[0, :, _cols(0, hd)], mv_ref[0, :, _cols(0, hd)])
                       for hd in range(XA_HEADS)])
    for hd in range(XA_HEADS):
        mix_ref[0, :, _cols(_MIX_XA, hd)] = (ao[hd] * _silu(ag[:, _cols(0, hd)])).astype(_BF16)


def _sample_kernel(x_ref, cos_ref, sin_ref, mk_ref, mv_ref, s0_ref, gnorm_ref, win_ref, gret_ref, ggm_ref, ws_ref,
                   bs_ref, dmask_ref, qdec_ref, kdec_ref, sdec_ref, mix_ref, s_ref, gvn_ref, h_s, proj_s,
                   *, ts, sp):
    j = pl.program_id(1)

    @pl.when(j == 0)
    def _():
        h_s[...] = (_rms_scale(x_ref[0]) * gnorm_ref[...]).astype(_BF16)
        slab = 4 * HEAD_DIM
        for c0 in range(0, IN_WIDTH, slab):
            proj_s[:, c0:c0 + slab] = _dot(h_s[...], win_ref[:, c0:c0 + slab])
        cos = cos_ref[...]
        sin = sin_ref[...]
        for head in range(RET_HEADS):
            proj_s[:, _cols(_RQ, head)] = _rope(proj_s[:, _cols(_RQ, head)], cos, sin)
            proj_s[:, _cols(_RK, head)] = _rope(proj_s[:, _cols(_RK, head)], cos, sin) * (HEAD_DIM ** -0.5)
        gvn = _center_scale(proj_s[:, _GV:_GV + GMLP_WIDTH]) * ggm_ref[...]
        gvn_ref[0] = gvn
        proj_s[:, _GV:_GV + GMLP_WIDTH] = gvn

    rows = [pl.ds(pl.multiple_of((j * sp + i) * ts, ts), ts) for i in range(sp)]

    units = [(i, head) for i in range(sp) for head in range(RET_HEADS)]
    o, new_states = _retention_blocks(
        [(proj_s[rows[i], _cols(_RQ, head)].astype(_BF16), proj_s[rows[i], _cols(_RK, head)],
          proj_s[rows[i], _cols(_RV, head)].astype(_BF16), s0_ref[i, head], head) for i, head in units],
        dmask_ref, qdec_ref, kdec_ref, sdec_ref)
    for (i, head), o_u, s_u in zip(units, o, new_states):
        s_ref[i, head] = s_u
        on = _center_scale(o_u) * gret_ref[:, _cols(0, head)]
        mix_ref[0, rows[i], _cols(0, head)] = (on * _silu(proj_s[rows[i], _cols(_RG, head)])).astype(_BF16)

    for g in range(GMLP_GROUPS):
        wm = _tril_bf16(ws_ref[g])
        for i in range(sp):
            sg = _dot(wm, proj_s[rows[i], _cols(_GV, g)].astype(_BF16)) + bs_ref[g]
            mix_ref[0, rows[i], _cols(_MIX_GMLP, g)] = (
                proj_s[rows[i], _cols(_GU, g)] * sg * _silu(proj_s[rows[i], _cols(_GG, g)])).astype(_BF16)

    units = [(i, hd) for i in range(sp) for hd in range(XA_HEADS)]
    ao = _xattn_heads([(proj_s[rows[i], _cols(_AQ, hd)], mk_ref[i, hd], mv_ref[i, hd]) for i, hd in units])
    for (i, hd), ao_u in zip(units, ao):
        mix_ref[0, rows[i], _cols(_MIX_XA, hd)] = (ao_u * _silu(proj_s[rows[i], _cols(_AG, hd)])).astype(_BF16)


def _outproj_kernel(x_ref, mix_ref, w_ref, g_ref, y_ref):
    y = x_ref[...] + _dot(mix_ref[...], w_ref[...])
    y_ref[...] = _rms_scale(y) * g_ref[...]


def _const_spec(shape):
    return pl.BlockSpec(shape, lambda *_: (0,) * len(shape), pipeline_mode=pl.Buffered(1))


def _rope_tables(pos):
    inv_freq = ROPE_BASE ** (-np.arange(0, HEAD_DIM, 2, dtype=np.float64) / HEAD_DIM)
    ang = np.asarray(pos, np.float64)[:, None] * inv_freq[None, :]
    cos, sin = np.cos(ang), np.sin(ang)
    return (np.concatenate([cos, cos], axis=-1).astype(np.float32),
            np.concatenate([-sin, sin], axis=-1).astype(np.float32))


def _decay_tables(lb):
    log_gamma = np.log(1.0 - 2.0 ** (-5.0 - np.arange(RET_HEADS, dtype=np.float64)))
    idx = np.arange(lb, dtype=np.float64)
    diff = idx[:, None] - idx[None, :]
    dmask = np.where(diff >= 0, np.exp(log_gamma[:, None, None] * np.maximum(diff, 0.0)[None]), 0.0)
    qdec = np.exp(log_gamma[:, None] * (idx[None, :] + 1.0))
    kdec = np.exp(log_gamma[:, None] * (lb - 1.0 - idx[None, :]))
    sdec = np.exp(log_gamma * lb)
    bcast = lambda a: np.broadcast_to(a[..., None], a.shape + (HEAD_DIM,)).astype(np.float32)
    return [dmask.astype(np.float32), bcast(qdec), bcast(kdec), bcast(sdec)[:, None, :]]


def _branch_consts(weights, gblk, lb):
    g_norm, w_in16, g_ret, g_gmlp, w_s, b_s = weights
    bs_blk = jnp.broadcast_to(b_s[:, :gblk, None], (GMLP_GROUPS, gblk, HEAD_DIM))
    return [g_norm.reshape(1, -1), w_in16, g_ret.reshape(1, -1), g_gmlp.reshape(1, -1), w_s[:, :gblk, :gblk],
            bs_blk] + _decay_tables(lb)


def _layer_prompt(x, mk16, mv16, weights, *, ts, lb):
    nb, l, _ = x.shape
    tile = lambda width: pl.BlockSpec((1, ts, width), lambda b, t: (b, t, 0))
    per_stream = lambda shape: pl.BlockSpec(shape, lambda b, t: (b,) + (0,) * (len(shape) - 1))
    rope_spec = pl.BlockSpec((ts, HEAD_DIM), lambda b, t: (t, 0))
    state_shape = (1, RET_HEADS, HEAD_DIM, HEAD_DIM)
    kv_spec = per_stream((1, MEM_LEN, XA_WIDTH))
    consts = _branch_consts(weights, GMLP_CHUNK, lb)
    return pl.pallas_call(
        functools.partial(_prompt_kernel, lb=lb),
        grid=(nb, l // ts),
        in_specs=[tile(D_MODEL), rope_spec, rope_spec, kv_spec, kv_spec] + [_const_spec(c.shape) for c in consts],
        out_specs=[tile(MIX_WIDTH), per_stream(state_shape)],
        out_shape=[jax.ShapeDtypeStruct((nb, l, MIX_WIDTH), _BF16),
                   jax.ShapeDtypeStruct((nb,) + state_shape[1:], _F32)],
        scratch_shapes=[pltpu.VMEM((ts, D_MODEL), _BF16)],
        compiler_params=pltpu.CompilerParams(
            dimension_semantics=("arbitrary", "arbitrary"), vmem_limit_bytes=_V7X_VMEM_LIMIT_BYTES),
        name="layer_prompt",
    )(x, *_rope_tables(np.arange(l)), mk16, mv16, *consts)


def _layer_sample(x, cache_k, cache_v, s0, weights, *, group, sp):
    n, ts, _ = x.shape
    m = group * ts
    ng = n // group
    steps = group // sp
    xg = x.reshape(ng, m, D_MODEL)
    grp = lambda width: pl.BlockSpec((1, m, width), lambda g, j: (g, 0, 0))
    per_step = lambda shape: pl.BlockSpec(shape, lambda g, j: (g * steps + j,) + (0,) * (len(shape) - 1))
    state_shape = (sp, RET_HEADS, HEAD_DIM, HEAD_DIM)
    kv_spec = per_step((sp, XA_HEADS, MEM_LEN, HEAD_DIM))
    rope_spec = _const_spec((m, HEAD_DIM))
    consts = _branch_consts(weights, ts, ts)
    mix, s_new, gvn = pl.pallas_call(
        functools.partial(_sample_kernel, ts=ts, sp=sp),
        grid=(ng, steps),
        in_specs=[grp(D_MODEL), rope_spec, rope_spec, kv_spec, kv_spec, per_step(state_shape)]
        + [_const_spec(c.shape) for c in consts],
        out_specs=[grp(MIX_WIDTH), per_step(state_shape), grp(GMLP_WIDTH)],
        out_shape=[jax.ShapeDtypeStruct((ng, m, MIX_WIDTH), _BF16),
                   jax.ShapeDtypeStruct((n,) + state_shape[1:], _F32),
                   jax.ShapeDtypeStruct((ng, m, GMLP_WIDTH), _F32)],
        scratch_shapes=[pltpu.VMEM((m, D_MODEL), _BF16), pltpu.VMEM((m, IN_WIDTH), _F32)],
        compiler_params=pltpu.CompilerParams(
            dimension_semantics=("arbitrary", "arbitrary"), vmem_limit_bytes=_V7X_VMEM_LIMIT_BYTES),
        name="layer_sample",
    )(xg, *_rope_tables(np.tile(PAST_LEN + np.arange(ts), group)), cache_k, cache_v, s0, *consts)
    return mix.reshape(n * ts, MIX_WIDTH), s_new, gvn.reshape(n, ts, GMLP_WIDTH)


def _out_proj(x, mix, w_out16, g_final, *, tile_rows):
    rows = x.shape[0]
    blk = pl.BlockSpec((tile_rows, D_MODEL), lambda i: (i, 0))
    return pl.pallas_call(
        _outproj_kernel,
        grid=(rows // tile_rows,),
        in_specs=[blk, blk, _const_spec(w_out16.shape), _const_spec((1, D_MODEL))],
        out_specs=blk,
        out_shape=jax.ShapeDtypeStruct(x.shape, _F32),
        compiler_params=pltpu.CompilerParams(
            dimension_semantics=("arbitrary",), vmem_limit_bytes=_V7X_VMEM_LIMIT_BYTES),
        name="out_proj",
    )(x, mix, w_out16, g_final.reshape(1, -1))


def _memory_kv(mem, g_mem, w_mem_kv16):
    b = mem.shape[0]
    blk = lambda width: pl.BlockSpec((1, MEM_LEN, width), lambda i: (i, 0, 0))
    blk4 = pl.BlockSpec((1, MEM_LEN, XA_HEADS, HEAD_DIM), lambda i: (i, 0, 0, 0))
    kv_f32 = jax.ShapeDtypeStruct((b, MEM_LEN, XA_HEADS, HEAD_DIM), _F32)
    kv_b16 = jax.ShapeDtypeStruct((b, MEM_LEN, XA_WIDTH), _BF16)
    return pl.pallas_call(
        _memkv_kernel,
        grid=(b,),
        in_specs=[blk(D_MODEL), _const_spec((1, D_MODEL)), _const_spec(w_mem_kv16.shape)],
        out_specs=[blk4, blk4, blk(XA_WIDTH), blk(XA_WIDTH)],
        out_shape=[kv_f32, kv_f32, kv_b16, kv_b16],
        compiler_params=pltpu.CompilerParams(dimension_semantics=("arbitrary",)),
        name="memory_kv",
    )(mem, g_mem.reshape(1, -1), w_mem_kv16)


PROMPT_TILE = 512
RET_BLOCK = 256
OUT_TILE = 512
SAMPLE_GROUP = 8
SAMPLE_STEP = 4


def kernel(x_prompt, x_sample, mem_prompt, state_ret, cache_mem_k, cache_mem_v, g_norm, w_in, g_ret, g_gmlp,
           w_s, b_s, g_mem, w_mem_kv, w_out, g_final):
    assert g_norm.shape[0] == 1
    b_p, l_p, _ = x_prompt.shape
    b_s_, l_s, _ = x_sample.shape
    w_out16 = w_out[0].astype(_BF16)
    weights = (g_norm[0], w_in[0].astype(_BF16), g_ret[0], g_gmlp[0], w_s[0], b_s[0])

    mk, mv, mk16, mv16 = _memory_kv(mem_prompt, g_mem[0], w_mem_kv[0].astype(_BF16))

    mix_p, s_p = _layer_prompt(x_prompt, mk16, mv16, weights, ts=PROMPT_TILE, lb=RET_BLOCK)
    y_p = _out_proj(x_prompt.reshape(b_p * l_p, D_MODEL), mix_p.reshape(b_p * l_p, MIX_WIDTH), w_out16, g_final,
                    tile_rows=OUT_TILE)

    head_major16 = lambda c: jnp.transpose(c[0], (0, 2, 1, 3)).astype(_BF16)
    mix_s, s_s, gvn_s = _layer_sample(x_sample, head_major16(cache_mem_k), head_major16(cache_mem_v), state_ret[0],
                                      weights, group=SAMPLE_GROUP, sp=SAMPLE_STEP)
    y_s = _out_proj(x_sample.reshape(b_s_ * l_s, D_MODEL), mix_s, w_out16, g_final, tile_rows=b_s_ * l_s)

    return (y_p.reshape(x_prompt.shape), y_s.reshape(x_sample.shape), s_p[None], mk[None], mv[None],
            s_s[None], gvn_s[None])
```

```python
import functools

import numpy as np

import jax
import jax.numpy as jnp
from jax import lax
from jax.experimental import pallas as pl
from jax.experimental.pallas import tpu as pltpu

D_MODEL = 2048
PAST_LEN = 1024
MEM_LEN = 256
RET_HEADS = 8
HEAD_DIM = 128
RET_WIDTH = RET_HEADS * HEAD_DIM
GMLP_GROUPS = 4
GMLP_WIDTH = GMLP_GROUPS * HEAD_DIM
GMLP_CHUNK = 128
XA_HEADS = 4
XA_WIDTH = XA_HEADS * HEAD_DIM
MIX_WIDTH = RET_WIDTH + GMLP_WIDTH + XA_WIDTH
ROPE_BASE = 10000.0
EPS = 1e-6

_RQ, _RK, _RV, _RG = 0, RET_WIDTH, 2 * RET_WIDTH, 3 * RET_WIDTH
_GU = 4 * RET_WIDTH
_GV = _GU + GMLP_WIDTH
_GG = _GV + GMLP_WIDTH
_AQ = _GG + GMLP_WIDTH
_AG = _AQ + XA_WIDTH
IN_WIDTH = _AG + XA_WIDTH
W_SLAB = 4 * HEAD_DIM
_MIX_GMLP = RET_WIDTH
_MIX_XA = RET_WIDTH + GMLP_WIDTH

_V7X_VMEM_LIMIT_BYTES = 60 * 1024 * 1024

_BF16 = jnp.bfloat16
_F32 = jnp.float32


def _dot(a, b):
    return jnp.dot(a, b, preferred_element_type=_F32)


def _dot_nt(a, b):
    return lax.dot_general(a, b, (((1,), (1,)), ((), ())), preferred_element_type=_F32)


def _dot_tn(a, b):
    return lax.dot_general(a, b, (((0,), (0,)), ((), ())), preferred_element_type=_F32)


def _silu(x):
    return x / (1.0 + jnp.exp(-x))


def _rms_scale(x):
    return x * lax.rsqrt(jnp.mean(x * x, axis=-1, keepdims=True) + EPS)


def _center_scale(x):
    mu = jnp.mean(x, axis=-1, keepdims=True)
    d = x - mu
    return d * lax.rsqrt(jnp.mean(d * d, axis=-1, keepdims=True) + EPS)


def _rope(u, cos2, sin2):
    return u * cos2 + pltpu.roll(u, HEAD_DIM // 2, 1) * sin2


def _cols(base, i):
    return slice(base + i * HEAD_DIM, base + (i + 1) * HEAD_DIM)


def _retention_blocks(blocks, dmask_ref, qdec_ref, kdec_ref, sdec_ref):
    sc = [_dot_nt(qb, kb.astype(_BF16)) for qb, kb, _, _, _ in blocks]
    cross = [_dot(qb, state.astype(_BF16)) for qb, _, _, state, _ in blocks]
    new_states = [state * sdec_ref[head] + _dot_tn((kb * kdec_ref[head]).astype(_BF16), vb)
                  for _, kb, vb, state, head in blocks]
    o = [_dot((s * dmask_ref[head]).astype(_BF16), vb) + c * qdec_ref[head]
         for s, c, (_, _, vb, _, head) in zip(sc, cross, blocks)]
    return o, new_states


def _tril_bf16(w):
    n = w.shape[0]
    row = lax.broadcasted_iota(jnp.int32, (n, n), 0)
    col = lax.broadcasted_iota(jnp.int32, (n, n), 1)
    return jnp.where(row >= col, w, 0.0).astype(_BF16)


def _xattn_heads(heads):
    sc = [_dot_nt(aq.astype(_BF16), mk) * (HEAD_DIM ** -0.5) for aq, mk, _ in heads]
    e = [jnp.exp(s - jnp.max(s, axis=-1, keepdims=True)) for s in sc]
    return [_dot(p.astype(_BF16), mv) / jnp.sum(p, axis=-1, keepdims=True) for p, (_, _, mv) in zip(e, heads)]


def _memkv_kernel(mem_ref, g_ref, w_ref, k_ref, v_ref, k16_ref, v16_ref, w16_s):
    @pl.when(pl.program_id(0) == 0)
    def _():
        w16_s[...] = w_ref[...].astype(_BF16)

    n = (_rms_scale(mem_ref[0]) * g_ref[...]).astype(_BF16)
    kv = _dot(n, w16_s[...])
    k = kv[:, :XA_WIDTH]
    v = kv[:, XA_WIDTH:]
    for hd in range(XA_HEADS):
        k_ref[0, :, hd, :] = k[:, _cols(0, hd)]
        v_ref[0, :, hd, :] = v[:, _cols(0, hd)]
    k16_ref[0] = k.astype(_BF16)
    v16_ref[0] = v.astype(_BF16)


def _prompt_kernel(x_ref, cos_ref, sin_ref, mk_ref, mv_ref, gnorm_ref, win_ref, gret_ref, ggm_ref, ws_ref, bs_ref,
                   dmask_ref, qdec_ref, kdec_ref, sdec_ref, mix_ref, s_ref, h_s, *, lb):
    ts = x_ref.shape[1]

    @pl.when(pl.program_id(1) == 0)
    def _():
        s_ref[...] = jnp.zeros_like(s_ref)

    h_s[...] = (_rms_scale(x_ref[0]) * gnorm_ref[...]).astype(_BF16)

    def proj(c0, width):
        return _dot(h_s[...], win_ref[:, c0:c0 + width])

    cos = cos_ref[...]
    sin = sin_ref[...]

    decay = (dmask_ref, qdec_ref, kdec_ref, sdec_ref)

    def retention_pair(p):
        c0 = 2 * HEAD_DIM * p
        q2, k2, v2, g2 = (proj(sec + c0, 2 * HEAD_DIM) for sec in (_RQ, _RK, _RV, _RG))
        heads = (2 * p, 2 * p + 1)
        qh = [_rope(q2[:, _cols(0, hh)], cos, sin).astype(_BF16) for hh in range(2)]
        kh = [_rope(k2[:, _cols(0, hh)], cos, sin) * (HEAD_DIM ** -0.5) for hh in range(2)]
        vh = [v2[:, _cols(0, hh)].astype(_BF16) for hh in range(2)]
        for bi in range(ts // lb):
            rs = slice(bi * lb, (bi + 1) * lb)
            o, new_states = _retention_blocks(
                [(qh[hh][rs], kh[hh][rs], vh[hh][rs], s_ref[0, head], head) for hh, head in enumerate(heads)],
                *decay)
            for hh, head in enumerate(heads):
                s_ref[0, head] = new_states[hh]
                on = _center_scale(o[hh]) * gret_ref[:, _cols(0, head)]
                mix_ref[0, rs, _cols(0, head)] = (on * _silu(g2[rs, _cols(0, hh)])).astype(_BF16)

    def gmlp():
        gu = proj(_GU, GMLP_WIDTH)
        gvn = _center_scale(proj(_GV, GMLP_WIDTH)) * ggm_ref[...]
        gg = proj(_GG, GMLP_WIDTH)
        for g in range(GMLP_GROUPS):
            hs = _cols(0, g)
            wm = _tril_bf16(ws_ref[g])
            for c in range(ts // GMLP_CHUNK):
                rs = slice(c * GMLP_CHUNK, (c + 1) * GMLP_CHUNK)
                sg = _dot(wm, gvn[rs, hs].astype(_BF16)) + bs_ref[g]
                mix_ref[0, rs, _cols(_MIX_GMLP, g)] = (gu[rs, hs] * sg * _silu(gg[rs, hs])).astype(_BF16)

    def xattn():
        aq = proj(_AQ, XA_WIDTH)
        ag = proj(_AG, XA_WIDTH)
        ao = _xattn_heads([(aq[:, _cols(0, hd)], mk_ref[0, :, _cols(0, hd)], mv_ref[0, :, _cols(0, hd)])
                           for hd in range(XA_HEADS)])
        for hd in range(XA_HEADS):
            mix_ref[0, :, _cols(_MIX_XA, hd)] = (ao[hd] * _silu(ag[:, _cols(0, hd)])).astype(_BF16)

    for p in range(RET_HEADS // 2):
        retention_pair(p)
    gmlp()
    xattn()


class _SlabView:
    def __init__(self, ref):
        self.ref = ref

    def _at(self, rows, base, i):
        col = base + i * HEAD_DIM
        return (col // W_SLAB, rows, slice(col % W_SLAB, col % W_SLAB + HEAD_DIM))

    def get(self, rows, base, i):
        return self.ref[self._at(rows, base, i)]

    def set(self, rows, base, i, value):
        self.ref[self._at(rows, base, i)] = value


def _sample_kernel(x_ref, cos_ref, sin_ref, mk_ref, mv_ref, s0_ref, gnorm_ref, w_ref, gret_ref, ggm_ref, ws_ref,
                   bs_ref, dmask_ref, qdec_ref, kdec_ref, sdec_ref, w16_ref, mix_ref, s_ref, gvn_ref, h_s, proj_s,
                   *, ts, sp, n_slabs):
    n = pl.program_id(0)
    proj = _SlabView(proj_s)
    all_rows = slice(None)

    @pl.when(n == 0)
    def _():
        h_s[...] = (_rms_scale(x_ref[...]) * gnorm_ref[...]).astype(_BF16)

    @pl.when(n < n_slabs)
    def _():
        w16 = w_ref[...].astype(_BF16)
        w16_ref[...] = w16
        proj_s[n] = _dot(h_s[...], w16)

    @pl.when(n == n_slabs - 1)
    def _():
        cos = cos_ref[...]
        sin = sin_ref[...]
        for head in range(RET_HEADS):
            proj.set(all_rows, _RQ, head, _rope(proj.get(all_rows, _RQ, head), cos, sin))
            proj.set(all_rows, _RK, head, _rope(proj.get(all_rows, _RK, head), cos, sin) * (HEAD_DIM ** -0.5))
        gv = jnp.concatenate([proj.get(all_rows, _GV, g) for g in range(GMLP_GROUPS)], axis=1)
        gvn = _center_scale(gv) * ggm_ref[...]
        gvn_ref[...] = gvn
        for g in range(GMLP_GROUPS):
            proj.set(all_rows, _GV, g, gvn[:, _cols(0, g)])

    @pl.when(n >= n_slabs)
    def _():
        j = n - n_slabs
        rows = [pl.ds(pl.multiple_of((j * sp + i) * ts, ts), ts) for i in range(sp)]

        units = [(i, head) for i in range(sp) for head in range(RET_HEADS)]
        o, new_states = _retention_blocks(
            [(proj.get(rows[i], _RQ, head).astype(_BF16), proj.get(rows[i], _RK, head),
              proj.get(rows[i], _RV, head).astype(_BF16), s0_ref[i, head], head) for i, head in units],
            dmask_ref, qdec_ref, kdec_ref, sdec_ref)
        for (i, head), o_u, s_u in zip(units, o, new_states):
            s_ref[i, head] = s_u
            on = _center_scale(o_u) * gret_ref[:, _cols(0, head)]
            mix_ref[rows[i], _cols(0, head)] = (on * _silu(proj.get(rows[i], _RG, head))).astype(_BF16)

        for g in range(GMLP_GROUPS):
            wm = _tril_bf16(ws_ref[g])
            for i in range(sp):
                sg = _dot(wm, proj.get(rows[i], _GV, g).astype(_BF16)) + bs_ref[g]
                mix_ref[rows[i], _cols(_MIX_GMLP, g)] = (
                    proj.get(rows[i], _GU, g) * sg * _silu(proj.get(rows[i], _GG, g))).astype(_BF16)

        units = [(i, hd) for i in range(sp) for hd in range(XA_HEADS)]
        ao = _xattn_heads([(proj.get(rows[i], _AQ, hd), mk_ref[i, hd], mv_ref[i, hd]) for i, hd in units])
        for (i, hd), ao_u in zip(units, ao):
            mix_ref[rows[i], _cols(_MIX_XA, hd)] = (ao_u * _silu(proj.get(rows[i], _AG, hd))).astype(_BF16)


def _outproj_kernel(x_ref, mix_ref, w_ref, g_ref, y_ref):
    y = x_ref[...] + _dot(mix_ref[...], w_ref[...])
    y_ref[...] = _rms_scale(y) * g_ref[...]


def _const_spec(shape):
    return pl.BlockSpec(shape, lambda *_: (0,) * len(shape), pipeline_mode=pl.Buffered(1))


def _rope_tables(pos):
    inv_freq = ROPE_BASE ** (-np.arange(0, HEAD_DIM, 2, dtype=np.float64) / HEAD_DIM)
    ang = np.asarray(pos, np.float64)[:, None] * inv_freq[None, :]
    cos, sin = np.cos(ang), np.sin(ang)
    return (np.concatenate([cos, cos], axis=-1).astype(np.float32),
            np.concatenate([-sin, sin], axis=-1).astype(np.float32))


def _decay_tables(lb):
    log_gamma = np.log(1.0 - 2.0 ** (-5.0 - np.arange(RET_HEADS, dtype=np.float64)))
    idx = np.arange(lb, dtype=np.float64)
    diff = idx[:, None] - idx[None, :]
    dmask = np.where(diff >= 0, np.exp(log_gamma[:, None, None] * np.maximum(diff, 0.0)[None]), 0.0)
    qdec = np.exp(log_gamma[:, None] * (idx[None, :] + 1.0))
    kdec = np.exp(log_gamma[:, None] * (lb - 1.0 - idx[None, :]))
    sdec = np.exp(log_gamma * lb)
    bcast = lambda a: np.broadcast_to(a[..., None], a.shape + (HEAD_DIM,)).astype(np.float32)
    return [dmask.astype(np.float32), bcast(qdec), bcast(kdec), bcast(sdec)[:, None, :]]


def _branch_consts(weights, gblk, lb):
    g_ret, g_gmlp, w_s, b_s = weights
    bs_blk = jnp.broadcast_to(b_s[:, :gblk, None], (GMLP_GROUPS, gblk, HEAD_DIM))
    return [g_ret.reshape(1, -1), g_gmlp.reshape(1, -1), w_s[:, :gblk, :gblk], bs_blk] + _decay_tables(lb)


def _layer_prompt(x, mk16, mv16, g_norm, w_in16, weights, *, ts, lb):
    nb, l, _ = x.shape
    tile = lambda width: pl.BlockSpec((1, ts, width), lambda b, t: (b, t, 0))
    per_stream = lambda shape: pl.BlockSpec(shape, lambda b, t: (b,) + (0,) * (len(shape) - 1))
    rope_spec = pl.BlockSpec((ts, HEAD_DIM), lambda b, t: (t, 0))
    state_shape = (1, RET_HEADS, HEAD_DIM, HEAD_DIM)
    kv_spec = per_stream((1, MEM_LEN, XA_WIDTH))
    consts = [g_norm.reshape(1, -1), w_in16] + _branch_consts(weights, GMLP_CHUNK, lb)
    return pl.pallas_call(
        functools.partial(_prompt_kernel, lb=lb),
        grid=(nb, l // ts),
        in_specs=[tile(D_MODEL), rope_spec, rope_spec, kv_spec, kv_spec] + [_const_spec(c.shape) for c in consts],
        out_specs=[tile(MIX_WIDTH), per_stream(state_shape)],
        out_shape=[jax.ShapeDtypeStruct((nb, l, MIX_WIDTH), _BF16),
                   jax.ShapeDtypeStruct((nb,) + state_shape[1:], _F32)],
        scratch_shapes=[pltpu.VMEM((ts, D_MODEL), _BF16)],
        compiler_params=pltpu.CompilerParams(
            dimension_semantics=("arbitrary", "arbitrary"), vmem_limit_bytes=_V7X_VMEM_LIMIT_BYTES),
        name="layer_prompt",
    )(x, *_rope_tables(np.arange(l)), mk16, mv16, *consts)


def _layer_sample(x, cache_k, cache_v, s0, g_norm, w_in, weights, *, sp):
    n, ts, _ = x.shape
    m = n * ts
    n_slabs = IN_WIDTH // W_SLAB
    branch_step = lambda i: jnp.maximum(i - n_slabs, 0)
    per_step = lambda shape: pl.BlockSpec(shape, lambda i: (branch_step(i),) + (0,) * (len(shape) - 1))
    slab_spec = pl.BlockSpec((D_MODEL, W_SLAB), lambda i: (0, jnp.minimum(i, n_slabs - 1)))
    state_shape = (sp, RET_HEADS, HEAD_DIM, HEAD_DIM)
    kv_spec = per_step((sp, XA_HEADS, MEM_LEN, HEAD_DIM))
    consts = _branch_consts(weights, ts, ts)
    w_in16, mix, s_new, gvn = pl.pallas_call(
        functools.partial(_sample_kernel, ts=ts, sp=sp, n_slabs=n_slabs),
        grid=(n_slabs + n // sp,),
        in_specs=[_const_spec((m, D_MODEL)), _const_spec((m, HEAD_DIM)), _const_spec((m, HEAD_DIM)), kv_spec, kv_spec,
                  per_step(state_shape), _const_spec((1, D_MODEL)), slab_spec]
        + [_const_spec(c.shape) for c in consts],
        out_specs=[slab_spec, pl.BlockSpec((m, MIX_WIDTH), lambda i: (0, 0)), per_step(state_shape),
                   pl.BlockSpec((m, GMLP_WIDTH), lambda i: (0, 0))],
        out_shape=[jax.ShapeDtypeStruct(w_in.shape, _BF16),
                   jax.ShapeDtypeStruct((m, MIX_WIDTH), _BF16),
                   jax.ShapeDtypeStruct((n,) + state_shape[1:], _F32),
                   jax.ShapeDtypeStruct((m, GMLP_WIDTH), _F32)],
        scratch_shapes=[pltpu.VMEM((m, D_MODEL), _BF16), pltpu.VMEM((n_slabs, m, W_SLAB), _F32)],
        compiler_params=pltpu.CompilerParams(
            dimension_semantics=("arbitrary",), vmem_limit_bytes=_V7X_VMEM_LIMIT_BYTES),
        name="layer_sample",
    )(x.reshape(m, D_MODEL), *_rope_tables(np.tile(PAST_LEN + np.arange(ts), n)), cache_k, cache_v, s0,
      g_norm.reshape(1, -1), w_in, *consts)
    return w_in16, mix, s_new, gvn.reshape(n, ts, GMLP_WIDTH)


def _out_proj(x, mix, w_out16, g_final, *, tile_rows):
    rows = x.shape[0]
    blk = pl.BlockSpec((tile_rows, D_MODEL), lambda i: (i, 0))
    return pl.pallas_call(
        _outproj_kernel,
        grid=(rows // tile_rows,),
        in_specs=[blk, blk, _const_spec(w_out16.shape), _const_spec((1, D_MODEL))],
        out_specs=blk,
        out_shape=jax.ShapeDtypeStruct(x.shape, _F32),
        compiler_params=pltpu.CompilerParams(
            dimension_semantics=("arbitrary",), vmem_limit_bytes=_V7X_VMEM_LIMIT_BYTES),
        name="out_proj",
    )(x, mix, w_out16, g_final.reshape(1, -1))


def _memory_kv(mem, g_mem, w_mem_kv):
    b = mem.shape[0]
    blk = lambda width: pl.BlockSpec((1, MEM_LEN, width), lambda i: (i, 0, 0))
    blk4 = pl.BlockSpec((1, MEM_LEN, XA_HEADS, HEAD_DIM), lambda i: (i, 0, 0, 0))
    kv_f32 = jax.ShapeDtypeStruct((b, MEM_LEN, XA_HEADS, HEAD_DIM), _F32)
    kv_b16 = jax.ShapeDtypeStruct((b, MEM_LEN, XA_WIDTH), _BF16)
    return pl.pallas_call(
        _memkv_kernel,
        grid=(b,),
        in_specs=[blk(D_MODEL), _const_spec((1, D_MODEL)), _const_spec(w_mem_kv.shape)],
        out_specs=[blk4, blk4, blk(XA_WIDTH), blk(XA_WIDTH)],
        out_shape=[kv_f32, kv_f32, kv_b16, kv_b16],
        scratch_shapes=[pltpu.VMEM(w_mem_kv.shape, _BF16)],
        compiler_params=pltpu.CompilerParams(dimension_semantics=("arbitrary",)),
        name="memory_kv",
    )(mem, g_mem.reshape(1, -1), w_mem_kv)


PROMPT_TILE = 512
RET_BLOCK = 256
OUT_TILE = 512
SAMPLE_STEP = 4


def kernel(x_prompt, x_sample, mem_prompt, state_ret, cache_mem_k, cache_mem_v, g_norm, w_in, g_ret, g_gmlp,
           w_s, b_s, g_mem, w_mem_kv, w_out, g_final):
    assert g_norm.shape[0] == 1
    b_p, l_p, _ = x_prompt.shape
    b_s_, l_s, _ = x_sample.shape
    w_out16 = w_out[0].astype(_BF16)
    weights = (g_ret[0], g_gmlp[0], w_s[0], b_s[0])

    mk, mv, mk16, mv16 = _memory_kv(mem_prompt, g_mem[0], w_mem_kv[0])

    head_major16 = lambda c: jnp.transpose(c[0], (0, 2, 1, 3)).astype(_BF16)
    w_in16, mix_s, s_s, gvn_s = _layer_sample(x_sample, head_major16(cache_mem_k), head_major16(cache_mem_v),
                                              state_ret[0], g_norm[0], w_in[0], weights, sp=SAMPLE_STEP)
    y_s = _out_proj(x_sample.reshape(b_s_ * l_s, D_MODEL), mix_s, w_out16, g_final, tile_rows=b_s_ * l_s)

    mix_p, s_p = _layer_prompt(x_prompt, mk16, mv16, g_norm[0], w_in16, weights, ts=PROMPT_TILE, lb=RET_BLOCK)
    y_p = _out_proj(x_prompt.reshape(b_p * l_p, D_MODEL), mix_p.reshape(b_p * l_p, MIX_WIDTH), w_out16, g_final,
                    tile_rows=OUT_TILE)

    return (y_p.reshape(x_prompt.shape), y_s.reshape(x_sample.shape), s_p[None], mk[None], mv[None],
            s_s[None], gvn_s[None])
```

```python
import functools

import numpy as np

import jax
import jax.numpy as jnp
from jax import lax
from jax.experimental import pallas as pl
from jax.experimental.pallas import tpu as pltpu

D_MODEL = 2048
PAST_LEN = 1024
MEM_LEN = 256
RET_HEADS = 8
HEAD_DIM = 128
RET_WIDTH = RET_HEADS * HEAD_DIM
GMLP_GROUPS = 4
GMLP_WIDTH = GMLP_GROUPS * HEAD_DIM
GMLP_CHUNK = 128
XA_HEADS = 4
XA_WIDTH = XA_HEADS * HEAD_DIM
MIX_WIDTH = RET_WIDTH + GMLP_WIDTH + XA_WIDTH
ROPE_BASE = 10000.0
EPS = 1e-6

_RQ, _RK, _RV, _RG = 0, RET_WIDTH, 2 * RET_WIDTH, 3 * RET_WIDTH
_GU = 4 * RET_WIDTH
_GV = _GU + GMLP_WIDTH
_GG = _GV + GMLP_WIDTH
_AQ = _GG + GMLP_WIDTH
_AG = _AQ + XA_WIDTH
IN_WIDTH = _AG + XA_WIDTH
W_SLAB = 4 * HEAD_DIM
WOUT_SLAB = 2 * HEAD_DIM
_MIX_GMLP = RET_WIDTH
_MIX_XA = RET_WIDTH + GMLP_WIDTH

_V7X_VMEM_LIMIT_BYTES = 60 * 1024 * 1024

_BF16 = jnp.bfloat16
_F32 = jnp.float32


def _dot(a, b):
    return jnp.dot(a, b, preferred_element_type=_F32)


def _dot_nt(a, b):
    return lax.dot_general(a, b, (((1,), (1,)), ((), ())), preferred_element_type=_F32)


def _dot_tn(a, b):
    return lax.dot_general(a, b, (((0,), (0,)), ((), ())), preferred_element_type=_F32)


def _silu(x):
    return x / (1.0 + jnp.exp(-x))


def _rms_scale(x):
    return x * lax.rsqrt(jnp.mean(x * x, axis=-1, keepdims=True) + EPS)


def _center_scale(x):
    mu = jnp.mean(x, axis=-1, keepdims=True)
    d = x - mu
    return d * lax.rsqrt(jnp.mean(d * d, axis=-1, keepdims=True) + EPS)


def _rope(u, cos2, sin2):
    return u * cos2 + pltpu.roll(u, HEAD_DIM // 2, 1) * sin2


def _cols(base, i):
    return slice(base + i * HEAD_DIM, base + (i + 1) * HEAD_DIM)


def _retention_blocks(blocks, dmask_ref, qdec_ref, kdec_ref, sdec_ref):
    sc = [_dot_nt(qb, kb.astype(_BF16)) for qb, kb, _, _, _ in blocks]
    cross = [_dot(qb, state.astype(_BF16)) for qb, _, _, state, _ in blocks]
    new_states = [state * sdec_ref[head] + _dot_tn((kb * kdec_ref[head]).astype(_BF16), vb)
                  for _, kb, vb, state, head in blocks]
    o = [_dot((s * dmask_ref[head]).astype(_BF16), vb) + c * qdec_ref[head]
         for s, c, (_, _, vb, _, head) in zip(sc, cross, blocks)]
    return o, new_states


def _tril_bf16(w):
    n = w.shape[0]
    row = lax.broadcasted_iota(jnp.int32, (n, n), 0)
    col = lax.broadcasted_iota(jnp.int32, (n, n), 1)
    return jnp.where(row >= col, w, 0.0).astype(_BF16)


def _xattn_heads(heads):
    sc = [_dot_nt(aq.astype(_BF16), mk) * (HEAD_DIM ** -0.5) for aq, mk, _ in heads]
    e = [jnp.exp(s - jnp.max(s, axis=-1, keepdims=True)) for s in sc]
    return [_dot(p.astype(_BF16), mv) / jnp.sum(p, axis=-1, keepdims=True) for p, (_, _, mv) in zip(e, heads)]


def _memkv_kernel(mem_ref, g_ref, w_ref, k_ref, v_ref, k16_ref, v16_ref, w16_s):
    @pl.when(pl.program_id(0) == 0)
    def _():
        w16_s[...] = w_ref[...].astype(_BF16)

    n = (_rms_scale(mem_ref[0]) * g_ref[...]).astype(_BF16)
    kv = _dot(n, w16_s[...])
    k = kv[:, :XA_WIDTH]
    v = kv[:, XA_WIDTH:]
    for hd in range(XA_HEADS):
        k_ref[0, :, hd, :] = k[:, _cols(0, hd)]
        v_ref[0, :, hd, :] = v[:, _cols(0, hd)]
    k16_ref[0] = k.astype(_BF16)
    v16_ref[0] = v.astype(_BF16)


def _prompt_kernel(x_ref, cos_ref, sin_ref, mk_ref, mv_ref, gnorm_ref, win_ref, gret_ref, ggm_ref, ws_ref, bs_ref,
                   dmask_ref, qdec_ref, kdec_ref, sdec_ref, mix_ref, s_ref, h_s, *, lb):
    ts = x_ref.shape[1]

    @pl.when(pl.program_id(1) == 0)
    def _():
        s_ref[...] = jnp.zeros_like(s_ref)

    h_s[...] = (_rms_scale(x_ref[0]) * gnorm_ref[...]).astype(_BF16)

    def proj(c0, width):
        return _dot(h_s[...], win_ref[:, c0:c0 + width])

    cos = cos_ref[...]
    sin = sin_ref[...]

    decay = (dmask_ref, qdec_ref, kdec_ref, sdec_ref)

    def retention_pair(p):
        c0 = 2 * HEAD_DIM * p
        q2, k2, v2, g2 = (proj(sec + c0, 2 * HEAD_DIM) for sec in (_RQ, _RK, _RV, _RG))
        heads = (2 * p, 2 * p + 1)
        qh = [_rope(q2[:, _cols(0, hh)], cos, sin).astype(_BF16) for hh in range(2)]
        kh = [_rope(k2[:, _cols(0, hh)], cos, sin) * (HEAD_DIM ** -0.5) for hh in range(2)]
        vh = [v2[:, _cols(0, hh)].astype(_BF16) for hh in range(2)]
        for bi in range(ts // lb):
            rs = slice(bi * lb, (bi + 1) * lb)
            o, new_states = _retention_blocks(
                [(qh[hh][rs], kh[hh][rs], vh[hh][rs], s_ref[0, head], head) for hh, head in enumerate(heads)],
                *decay)
            for hh, head in enumerate(heads):
                s_ref[0, head] = new_states[hh]
                on = _center_scale(o[hh]) * gret_ref[:, _cols(0, head)]
                mix_ref[0, rs, _cols(0, head)] = (on * _silu(g2[rs, _cols(0, hh)])).astype(_BF16)

    def gmlp():
        gu = proj(_GU, GMLP_WIDTH)
        gvn = _center_scale(proj(_GV, GMLP_WIDTH)) * ggm_ref[...]
        gg = proj(_GG, GMLP_WIDTH)
        for g in range(GMLP_GROUPS):
            hs = _cols(0, g)
            wm = _tril_bf16(ws_ref[g])
            for c in range(ts // GMLP_CHUNK):
                rs = slice(c * GMLP_CHUNK, (c + 1) * GMLP_CHUNK)
                sg = _dot(wm, gvn[rs, hs].astype(_BF16)) + bs_ref[g]
                mix_ref[0, rs, _cols(_MIX_GMLP, g)] = (gu[rs, hs] * sg * _silu(gg[rs, hs])).astype(_BF16)

    def xattn():
        aq = proj(_AQ, XA_WIDTH)
        ag = proj(_AG, XA_WIDTH)
        ao = _xattn_heads([(aq[:, _cols(0, hd)], mk_ref[0, :, _cols(0, hd)], mv_ref[0, :, _cols(0, hd)])
                           for hd in range(XA_HEADS)])
        for hd in range(XA_HEADS):
            mix_ref[0, :, _cols(_MIX_XA, hd)] = (ao[hd] * _silu(ag[:, _cols(0, hd)])).astype(_BF16)

    for p in range(RET_HEADS // 2):
        retention_pair(p)
    gmlp()
    xattn()


class _SlabView:
    def __init__(self, ref):
        self.ref = ref

    def _at(self, rows, base, i):
        col = base + i * HEAD_DIM
        return (col // W_SLAB, rows, slice(col % W_SLAB, col % W_SLAB + HEAD_DIM))

    def get(self, rows, base, i):
        return self.ref[self._at(rows, base, i)]

    def set(self, rows, base, i, value):
        self.ref[self._at(rows, base, i)] = value


def _sample_kernel(x_ref, cos_ref, sin_ref, mk_ref, mv_ref, s0_ref, gnorm_ref, w_ref, wout_ref, gret_ref, ggm_ref,
                   ws_ref, bs_ref, dmask_ref, qdec_ref, kdec_ref, sdec_ref, w16_ref, wout16_ref, mix_ref, s_ref,
                   gvn_ref, h_s, proj_s, *, ts, sp, n_slabs):
    n = pl.program_id(0)
    proj = _SlabView(proj_s)
    all_rows = slice(None)

    wout16_ref[...] = wout_ref[...].astype(_BF16)

    @pl.when(n == 0)
    def _():
        h_s[...] = (_rms_scale(x_ref[...]) * gnorm_ref[...]).astype(_BF16)

    @pl.when(n < n_slabs)
    def _():
        w16 = w_ref[...].astype(_BF16)
        w16_ref[...] = w16
        proj_s[n] = _dot(h_s[...], w16)

    @pl.when(n == n_slabs - 1)
    def _():
        cos = cos_ref[...]
        sin = sin_ref[...]
        for head in range(RET_HEADS):
            proj.set(all_rows, _RQ, head, _rope(proj.get(all_rows, _RQ, head), cos, sin))
            proj.set(all_rows, _RK, head, _rope(proj.get(all_rows, _RK, head), cos, sin) * (HEAD_DIM ** -0.5))
        gv = jnp.concatenate([proj.get(all_rows, _GV, g) for g in range(GMLP_GROUPS)], axis=1)
        gvn = _center_scale(gv) * ggm_ref[...]
        gvn_ref[...] = gvn
        for g in range(GMLP_GROUPS):
            proj.set(all_rows, _GV, g, gvn[:, _cols(0, g)])

    @pl.when(n >= n_slabs)
    def _():
        j = n - n_slabs
        rows = [pl.ds(pl.multiple_of((j * sp + i) * ts, ts), ts) for i in range(sp)]

        units = [(i, head) for i in range(sp) for head in range(RET_HEADS)]
        o, new_states = _retention_blocks(
            [(proj.get(rows[i], _RQ, head).astype(_BF16), proj.get(rows[i], _RK, head),
              proj.get(rows[i], _RV, head).astype(_BF16), s0_ref[i, head], head) for i, head in units],
            dmask_ref, qdec_ref, kdec_ref, sdec_ref)
        for (i, head), o_u, s_u in zip(units, o, new_states):
            s_ref[i, head] = s_u
            on = _center_scale(o_u) * gret_ref[:, _cols(0, head)]
            mix_ref[rows[i], _cols(0, head)] = (on * _silu(proj.get(rows[i], _RG, head))).astype(_BF16)

        for g in range(GMLP_GROUPS):
            wm = _tril_bf16(ws_ref[g])
            for i in range(sp):
                sg = _dot(wm, proj.get(rows[i], _GV, g).astype(_BF16)) + bs_ref[g]
                mix_ref[rows[i], _cols(_MIX_GMLP, g)] = (
                    proj.get(rows[i], _GU, g) * sg * _silu(proj.get(rows[i], _GG, g))).astype(_BF16)

        units = [(i, hd) for i in range(sp) for hd in range(XA_HEADS)]
        ao = _xattn_heads([(proj.get(rows[i], _AQ, hd), mk_ref[i, hd], mv_ref[i, hd]) for i, hd in units])
        for (i, hd), ao_u in zip(units, ao):
            mix_ref[rows[i], _cols(_MIX_XA, hd)] = (ao_u * _silu(proj.get(rows[i], _AG, hd))).astype(_BF16)


def _outproj_kernel(x_ref, mix_ref, w_ref, g_ref, y_ref):
    y = x_ref[...] + _dot(mix_ref[...], w_ref[...])
    y_ref[...] = _rms_scale(y) * g_ref[...]


def _const_spec(shape):
    return pl.BlockSpec(shape, lambda *_: (0,) * len(shape), pipeline_mode=pl.Buffered(1))


def _rope_tables(pos):
    inv_freq = ROPE_BASE ** (-np.arange(0, HEAD_DIM, 2, dtype=np.float64) / HEAD_DIM)
    ang = np.asarray(pos, np.float64)[:, None] * inv_freq[None, :]
    cos, sin = np.cos(ang), np.sin(ang)
    return (np.concatenate([cos, cos], axis=-1).astype(np.float32),
            np.concatenate([-sin, sin], axis=-1).astype(np.float32))


def _decay_tables(lb):
    log_gamma = np.log(1.0 - 2.0 ** (-5.0 - np.arange(RET_HEADS, dtype=np.float64)))
    idx = np.arange(lb, dtype=np.float64)
    diff = idx[:, None] - idx[None, :]
    dmask = np.where(diff >= 0, np.exp(log_gamma[:, None, None] * np.maximum(diff, 0.0)[None]), 0.0)
    qdec = np.exp(log_gamma[:, None] * (idx[None, :] + 1.0))
    kdec = np.exp(log_gamma[:, None] * (lb - 1.0 - idx[None, :]))
    sdec = np.exp(log_gamma * lb)
    bcast = lambda a: np.broadcast_to(a[..., None], a.shape + (HEAD_DIM,)).astype(np.float32)
    return [dmask.astype(np.float32), bcast(qdec), bcast(kdec), bcast(sdec)[:, None, :]]


def _branch_consts(weights, gblk, lb):
    g_ret, g_gmlp, w_s, b_s = weights
    bs_blk = jnp.broadcast_to(b_s[:, :gblk, None], (GMLP_GROUPS, gblk, HEAD_DIM))
    return [g_ret.reshape(1, -1), g_gmlp.reshape(1, -1), w_s[:, :gblk, :gblk], bs_blk] + _decay_tables(lb)


def _layer_prompt(x, mk16, mv16, g_norm, w_in16, weights, *, ts, lb):
    nb, l, _ = x.shape
    tile = lambda width: pl.BlockSpec((1, ts, width), lambda b, t: (b, t, 0))
    per_stream = lambda shape: pl.BlockSpec(shape, lambda b, t: (b,) + (0,) * (len(shape) - 1))
    rope_spec = pl.BlockSpec((ts, HEAD_DIM), lambda b, t: (t, 0))
    state_shape = (1, RET_HEADS, HEAD_DIM, HEAD_DIM)
    kv_spec = per_stream((1, MEM_LEN, XA_WIDTH))
    consts = [g_norm.reshape(1, -1), w_in16] + _branch_consts(weights, GMLP_CHUNK, lb)
    return pl.pallas_call(
        functools.partial(_prompt_kernel, lb=lb),
        grid=(nb, l // ts),
        in_specs=[tile(D_MODEL), rope_spec, rope_spec, kv_spec, kv_spec] + [_const_spec(c.shape) for c in consts],
        out_specs=[tile(MIX_WIDTH), per_stream(state_shape)],
        out_shape=[jax.ShapeDtypeStruct((nb, l, MIX_WIDTH), _BF16),
                   jax.ShapeDtypeStruct((nb,) + state_shape[1:], _F32)],
        scratch_shapes=[pltpu.VMEM((ts, D_MODEL), _BF16)],
        compiler_params=pltpu.CompilerParams(
            dimension_semantics=("arbitrary", "arbitrary"), vmem_limit_bytes=_V7X_VMEM_LIMIT_BYTES),
        name="layer_prompt",
    )(x, *_rope_tables(np.arange(l)), mk16, mv16, *consts)


def _layer_sample(x, cache_k, cache_v, s0, g_norm, w_in, w_out, weights, *, sp):
    n, ts, _ = x.shape
    m = n * ts
    n_slabs = IN_WIDTH // W_SLAB
    n_steps = n_slabs + n // sp
    n_out_slabs = D_MODEL // WOUT_SLAB
    branch_step = lambda i: jnp.maximum(i - n_slabs, 0)
    per_step = lambda shape: pl.BlockSpec(shape, lambda i: (branch_step(i),) + (0,) * (len(shape) - 1))
    slab_spec = pl.BlockSpec((D_MODEL, W_SLAB), lambda i: (0, jnp.minimum(i, n_slabs - 1)))
    out_slab_spec = pl.BlockSpec((MIX_WIDTH, WOUT_SLAB), lambda i: (0, jnp.maximum(i - (n_steps - n_out_slabs), 0)))
    state_shape = (sp, RET_HEADS, HEAD_DIM, HEAD_DIM)
    kv_spec = per_step((sp, XA_HEADS, MEM_LEN, HEAD_DIM))
    consts = _branch_consts(weights, ts, ts)
    w_in16, w_out16, mix, s_new, gvn = pl.pallas_call(
        functools.partial(_sample_kernel, ts=ts, sp=sp, n_slabs=n_slabs),
        grid=(n_steps,),
        in_specs=[_const_spec((m, D_MODEL)), _const_spec((m, HEAD_DIM)), _const_spec((m, HEAD_DIM)), kv_spec, kv_spec,
                  per_step(state_shape), _const_spec((1, D_MODEL)), slab_spec, out_slab_spec]
        + [_const_spec(c.shape) for c in consts],
        out_specs=[slab_spec, out_slab_spec, pl.BlockSpec((m, MIX_WIDTH), lambda i: (0, 0)), per_step(state_shape),
                   pl.BlockSpec((m, GMLP_WIDTH), lambda i: (0, 0))],
        out_shape=[jax.ShapeDtypeStruct(w_in.shape, _BF16),
                   jax.ShapeDtypeStruct(w_out.shape, _BF16),
                   jax.ShapeDtypeStruct((m, MIX_WIDTH), _BF16),
                   jax.ShapeDtypeStruct((n,) + state_shape[1:], _F32),
                   jax.ShapeDtypeStruct((m, GMLP_WIDTH), _F32)],
        scratch_shapes=[pltpu.VMEM((m, D_MODEL), _BF16), pltpu.VMEM((n_slabs, m, W_SLAB), _F32)],
        compiler_params=pltpu.CompilerParams(
            dimension_semantics=("arbitrary",), vmem_limit_bytes=_V7X_VMEM_LIMIT_BYTES),
        name="layer_sample",
    )(x.reshape(m, D_MODEL), *_rope_tables(np.tile(PAST_LEN + np.arange(ts), n)), cache_k, cache_v, s0,
      g_norm.reshape(1, -1), w_in, w_out, *consts)
    return w_in16, w_out16, mix, s_new, gvn.reshape(n, ts, GMLP_WIDTH)


def _out_proj(x, mix, w_out16, g_final, *, tile_rows):
    rows = x.shape[0]
    blk = pl.BlockSpec((tile_rows, D_MODEL), lambda i: (i, 0))
    return pl.pallas_call(
        _outproj_kernel,
        grid=(rows // tile_rows,),
        in_specs=[blk, blk, _const_spec(w_out16.shape), _const_spec((1, D_MODEL))],
        out_specs=blk,
        out_shape=jax.ShapeDtypeStruct(x.shape, _F32),
        compiler_params=pltpu.CompilerParams(
            dimension_semantics=("arbitrary",), vmem_limit_bytes=_V7X_VMEM_LIMIT_BYTES),
        name="out_proj",
    )(x, mix, w_out16, g_final.reshape(1, -1))


def _memory_kv(mem, g_mem, w_mem_kv):
    b = mem.shape[0]
    blk = lambda width: pl.BlockSpec((1, MEM_LEN, width), lambda i: (i, 0, 0))
    blk4 = pl.BlockSpec((1, MEM_LEN, XA_HEADS, HEAD_DIM), lambda i: (i, 0, 0, 0))
    kv_f32 = jax.ShapeDtypeStruct((b, MEM_LEN, XA_HEADS, HEAD_DIM), _F32)
    kv_b16 = jax.ShapeDtypeStruct((b, MEM_LEN, XA_WIDTH), _BF16)
    return pl.pallas_call(
        _memkv_kernel,
        grid=(b,),
        in_specs=[blk(D_MODEL), _const_spec((1, D_MODEL)), _const_spec(w_mem_kv.shape)],
        out_specs=[blk4, blk4, blk(XA_WIDTH), blk(XA_WIDTH)],
        out_shape=[kv_f32, kv_f32, kv_b16, kv_b16],
        scratch_shapes=[pltpu.VMEM(w_mem_kv.shape, _BF16)],
        compiler_params=pltpu.CompilerParams(dimension_semantics=("arbitrary",)),
        name="memory_kv",
    )(mem, g_mem.reshape(1, -1), w_mem_kv)


PROMPT_TILE = 512
RET_BLOCK = 256
OUT_TILE = 1024
SAMPLE_STEP = 4


def kernel(x_prompt, x_sample, mem_prompt, state_ret, cache_mem_k, cache_mem_v, g_norm, w_in, g_ret, g_gmlp,
           w_s, b_s, g_mem, w_mem_kv, w_out, g_final):
    assert g_norm.shape[0] == 1
    b_p, l_p, _ = x_prompt.shape
    b_s_, l_s, _ = x_sample.shape
    weights = (g_ret[0], g_gmlp[0], w_s[0], b_s[0])

    mk, mv, mk16, mv16 = _memory_kv(mem_prompt, g_mem[0], w_mem_kv[0])

    head_major16 = lambda c: jnp.transpose(c[0], (0, 2, 1, 3)).astype(_BF16)
    w_in16, w_out16, mix_s, s_s, gvn_s = _layer_sample(
        x_sample, head_major16(cache_mem_k), head_major16(cache_mem_v), state_ret[0], g_norm[0], w_in[0], w_out[0],
        weights, sp=SAMPLE_STEP)
    y_s = _out_proj(x_sample.reshape(b_s_ * l_s, D_MODEL), mix_s, w_out16, g_final, tile_rows=b_s_ * l_s)

    mix_p, s_p = _layer_prompt(x_prompt, mk16, mv16, g_norm[0], w_in16, weights, ts=PROMPT_TILE, lb=RET_BLOCK)
    y_p = _out_proj(x_prompt.reshape(b_p * l_p, D_MODEL), mix_p.reshape(b_p * l_p, MIX_WIDTH), w_out16, g_final,
                    tile_rows=OUT_TILE)

    return (y_p.reshape(x_prompt.shape), y_s.reshape(x_sample.shape), s_p[None], mk[None], mv[None],
            s_s[None], gvn_s[None])
```

```python
import functools

import numpy as np

import jax
import jax.numpy as jnp
from jax import lax
from jax.experimental import pallas as pl
from jax.experimental.pallas import tpu as pltpu

D_MODEL = 2048
PAST_LEN = 1024
MEM_LEN = 256
RET_HEADS = 8
HEAD_DIM = 128
RET_WIDTH = RET_HEADS * HEAD_DIM
GMLP_GROUPS = 4
GMLP_WIDTH = GMLP_GROUPS * HEAD_DIM
GMLP_CHUNK = 128
XA_HEADS = 4
XA_WIDTH = XA_HEADS * HEAD_DIM
MIX_WIDTH = RET_WIDTH + GMLP_WIDTH + XA_WIDTH
ROPE_BASE = 10000.0
EPS = 1e-6

_RQ, _RK, _RV, _RG = 0, RET_WIDTH, 2 * RET_WIDTH, 3 * RET_WIDTH
_GU = 4 * RET_WIDTH
_GV = _GU + GMLP_WIDTH
_GG = _GV + GMLP_WIDTH
_AQ = _GG + GMLP_WIDTH
_AG = _AQ + XA_WIDTH
IN_WIDTH = _AG + XA_WIDTH
W_SLAB = 4 * HEAD_DIM
WOUT_SLAB = HEAD_DIM
_MIX_GMLP = RET_WIDTH
_MIX_XA = RET_WIDTH + GMLP_WIDTH

_V7X_VMEM_LIMIT_BYTES = 60 * 1024 * 1024

_BF16 = jnp.bfloat16
_F32 = jnp.float32


def _dot(a, b):
    return jnp.dot(a, b, preferred_element_type=_F32)


def _dot_nt(a, b):
    return lax.dot_general(a, b, (((1,), (1,)), ((), ())), preferred_element_type=_F32)


def _dot_tn(a, b):
    return lax.dot_general(a, b, (((0,), (0,)), ((), ())), preferred_element_type=_F32)


def _silu(x):
    return x / (1.0 + jnp.exp(-x))


def _rms_scale(x):
    return x * lax.rsqrt(jnp.mean(x * x, axis=-1, keepdims=True) + EPS)


def _center_scale(x):
    mu = jnp.mean(x, axis=-1, keepdims=True)
    d = x - mu
    return d * lax.rsqrt(jnp.mean(d * d, axis=-1, keepdims=True) + EPS)


def _rope(u, cos2, sin2):
    return u * cos2 + pltpu.roll(u, HEAD_DIM // 2, 1) * sin2


def _cols(base, i):
    return slice(base + i * HEAD_DIM, base + (i + 1) * HEAD_DIM)


def _retention_blocks(blocks, dmask_ref, qdec_ref, kdec_ref, sdec_ref):
    sc = [_dot_nt(qb, kb.astype(_BF16)) for qb, kb, _, _, _ in blocks]
    cross = [_dot(qb, state.astype(_BF16)) for qb, _, _, state, _ in blocks]
    new_states = [state * sdec_ref[head] + _dot_tn((kb * kdec_ref[head]).astype(_BF16), vb)
                  for _, kb, vb, state, head in blocks]
    o = [_dot((s * dmask_ref[head]).astype(_BF16), vb) + c * qdec_ref[head]
         for s, c, (_, _, vb, _, head) in zip(sc, cross, blocks)]
    return o, new_states


def _tril_bf16(w):
    n = w.shape[0]
    row = lax.broadcasted_iota(jnp.int32, (n, n), 0)
    col = lax.broadcasted_iota(jnp.int32, (n, n), 1)
    return jnp.where(row >= col, w, 0.0).astype(_BF16)


def _xattn_heads(heads):
    sc = [_dot_nt(aq.astype(_BF16), mk) * (HEAD_DIM ** -0.5) for aq, mk, _ in heads]
    e = [jnp.exp(s - jnp.max(s, axis=-1, keepdims=True)) for s in sc]
    return [_dot(p.astype(_BF16), mv) / jnp.sum(p, axis=-1, keepdims=True) for p, (_, _, mv) in zip(e, heads)]


def _memkv_kernel(mem_ref, g_ref, w_ref, k_ref, v_ref, k16_ref, v16_ref, w16_s):
    @pl.when(pl.program_id(0) == 0)
    def _():
        w16_s[...] = w_ref[...].astype(_BF16)

    n = (_rms_scale(mem_ref[0]) * g_ref[...]).astype(_BF16)
    kv = _dot(n, w16_s[...])
    k = kv[:, :XA_WIDTH]
    v = kv[:, XA_WIDTH:]
    for hd in range(XA_HEADS):
        k_ref[0, :, hd, :] = k[:, _cols(0, hd)]
        v_ref[0, :, hd, :] = v[:, _cols(0, hd)]
    k16_ref[0] = k.astype(_BF16)
    v16_ref[0] = v.astype(_BF16)


def _prompt_kernel(x_ref, cos_ref, sin_ref, mk_ref, mv_ref, gnorm_ref, win_ref, gret_ref, ggm_ref, ws_ref, bs_ref,
                   dmask_ref, qdec_ref, kdec_ref, sdec_ref, wout_ref, mix_ref, s_ref, wout16_ref, h_s, *, lb):
    ts = x_ref.shape[1]
    wout16_ref[...] = wout_ref[...].astype(_BF16)

    @pl.when(pl.program_id(1) == 0)
    def _():
        s_ref[...] = jnp.zeros_like(s_ref)

    h_s[...] = (_rms_scale(x_ref[0]) * gnorm_ref[...]).astype(_BF16)

    def proj(c0, width):
        return _dot(h_s[...], win_ref[:, c0:c0 + width])

    cos = cos_ref[...]
    sin = sin_ref[...]

    decay = (dmask_ref, qdec_ref, kdec_ref, sdec_ref)

    def retention_pair(p):
        c0 = 2 * HEAD_DIM * p
        q2, k2, v2, g2 = (proj(sec + c0, 2 * HEAD_DIM) for sec in (_RQ, _RK, _RV, _RG))
        heads = (2 * p, 2 * p + 1)
        qh = [_rope(q2[:, _cols(0, hh)], cos, sin).astype(_BF16) for hh in range(2)]
        kh = [_rope(k2[:, _cols(0, hh)], cos, sin) * (HEAD_DIM ** -0.5) for hh in range(2)]
        vh = [v2[:, _cols(0, hh)].astype(_BF16) for hh in range(2)]
        for bi in range(ts // lb):
            rs = slice(bi * lb, (bi + 1) * lb)
            o, new_states = _retention_blocks(
                [(qh[hh][rs], kh[hh][rs], vh[hh][rs], s_ref[0, head], head) for hh, head in enumerate(heads)],
                *decay)
            for hh, head in enumerate(heads):
                s_ref[0, head] = new_states[hh]
                on = _center_scale(o[hh]) * gret_ref[:, _cols(0, head)]
                mix_ref[0, rs, _cols(0, head)] = (on * _silu(g2[rs, _cols(0, hh)])).astype(_BF16)

    def gmlp():
        gvn = _center_scale(proj(_GV, GMLP_WIDTH)) * ggm_ref[...]
        gg = proj(_GG, GMLP_WIDTH)
        gu = proj(_GU, GMLP_WIDTH)
        for g in range(GMLP_GROUPS):
            hs = _cols(0, g)
            wm = _tril_bf16(ws_ref[g])
            for c in range(ts // GMLP_CHUNK):
                rs = slice(c * GMLP_CHUNK, (c + 1) * GMLP_CHUNK)
                sg = _dot(wm, gvn[rs, hs].astype(_BF16)) + bs_ref[g]
                mix_ref[0, rs, _cols(_MIX_GMLP, g)] = (gu[rs, hs] * (sg * _silu(gg[rs, hs]))).astype(_BF16)

    def xattn():
        aq = proj(_AQ, XA_WIDTH)
        ag = proj(_AG, XA_WIDTH)
        ao = _xattn_heads([(aq[:, _cols(0, hd)], mk_ref[0, :, _cols(0, hd)], mv_ref[0, :, _cols(0, hd)])
                           for hd in range(XA_HEADS)])
        for hd in range(XA_HEADS):
            mix_ref[0, :, _cols(_MIX_XA, hd)] = (ao[hd] * _silu(ag[:, _cols(0, hd)])).astype(_BF16)

    for p in range(RET_HEADS // 2):
        retention_pair(p)
    xattn()
    gmlp()


class _SlabView:
    def __init__(self, ref):
        self.ref = ref

    def _at(self, rows, base, i):
        col = base + i * HEAD_DIM
        return (col // W_SLAB, rows, slice(col % W_SLAB, col % W_SLAB + HEAD_DIM))

    def get(self, rows, base, i):
        return self.ref[self._at(rows, base, i)]

    def set(self, rows, base, i, value):
        self.ref[self._at(rows, base, i)] = value


def _sample_kernel(x_ref, cos_ref, sin_ref, mk_ref, mv_ref, s0_ref, gnorm_ref, w_ref, gret_ref, ggm_ref, ws_ref,
                   bs_ref, dmask_ref, qdec_ref, kdec_ref, sdec_ref, w16_ref, mix_ref, s_ref, gvn_ref, h_s, proj_s,
                   *, ts, sp, n_slabs):
    n = pl.program_id(0)
    proj = _SlabView(proj_s)
    all_rows = slice(None)

    @pl.when(n == 0)
    def _():
        h_s[...] = (_rms_scale(x_ref[...]) * gnorm_ref[...]).astype(_BF16)

    @pl.when(n < n_slabs)
    def _():
        w16 = w_ref[...].astype(_BF16)
        w16_ref[...] = w16
        proj_s[n] = _dot(h_s[...], w16)

    @pl.when(n == n_slabs - 1)
    def _():
        cos = cos_ref[...]
        sin = sin_ref[...]
        for head in range(RET_HEADS):
            proj.set(all_rows, _RQ, head, _rope(proj.get(all_rows, _RQ, head), cos, sin))
            proj.set(all_rows, _RK, head, _rope(proj.get(all_rows, _RK, head), cos, sin) * (HEAD_DIM ** -0.5))
        gv = jnp.concatenate([proj.get(all_rows, _GV, g) for g in range(GMLP_GROUPS)], axis=1)
        gvn = _center_scale(gv) * ggm_ref[...]
        gvn_ref[...] = gvn
        for g in range(GMLP_GROUPS):
            proj.set(all_rows, _GV, g, gvn[:, _cols(0, g)])

    @pl.when(n >= n_slabs)
    def _():
        j = n - n_slabs
        rows = [pl.ds(pl.multiple_of((j * sp + i) * ts, ts), ts) for i in range(sp)]

        units = [(i, head) for i in range(sp) for head in range(RET_HEADS)]
        o, new_states = _retention_blocks(
            [(proj.get(rows[i], _RQ, head).astype(_BF16), proj.get(rows[i], _RK, head),
              proj.get(rows[i], _RV, head).astype(_BF16), s0_ref[i, head], head) for i, head in units],
            dmask_ref, qdec_ref, kdec_ref, sdec_ref)
        for (i, head), o_u, s_u in zip(units, o, new_states):
            s_ref[i, head] = s_u
            on = _center_scale(o_u) * gret_ref[:, _cols(0, head)]
            mix_ref[rows[i], _cols(0, head)] = (on * _silu(proj.get(rows[i], _RG, head))).astype(_BF16)

        for g in range(GMLP_GROUPS):
            wm = _tril_bf16(ws_ref[g])
            for i in range(sp):
                sg = _dot(wm, proj.get(rows[i], _GV, g).astype(_BF16)) + bs_ref[g]
                mix_ref[rows[i], _cols(_MIX_GMLP, g)] = (
                    proj.get(rows[i], _GU, g) * sg * _silu(proj.get(rows[i], _GG, g))).astype(_BF16)

        units = [(i, hd) for i in range(sp) for hd in range(XA_HEADS)]
        ao = _xattn_heads([(proj.get(rows[i], _AQ, hd), mk_ref[i, hd], mv_ref[i, hd]) for i, hd in units])
        for (i, hd), ao_u in zip(units, ao):
            mix_ref[rows[i], _cols(_MIX_XA, hd)] = (ao_u * _silu(proj.get(rows[i], _AG, hd))).astype(_BF16)


def _outproj_kernel(xp_ref, mixp_ref, xs_ref, mixs_ref, w_ref, g_ref, yp_ref, ys_ref, *, n_prompt_tiles):
    n = pl.program_id(0)

    def project(x_ref, mix_ref, y_ref):
        y = x_ref[...] + _dot(mix_ref[...], w_ref[...])
        y_ref[...] = _rms_scale(y) * g_ref[...]

    @pl.when(n < n_prompt_tiles)
    def _():
        project(xp_ref, mixp_ref, yp_ref)

    @pl.when(n == n_prompt_tiles)
    def _():
        project(xs_ref, mixs_ref, ys_ref)


def _const_spec(shape):
    return pl.BlockSpec(shape, lambda *_: (0,) * len(shape), pipeline_mode=pl.Buffered(1))


def _rope_tables(pos):
    inv_freq = ROPE_BASE ** (-np.arange(0, HEAD_DIM, 2, dtype=np.float64) / HEAD_DIM)
    ang = np.asarray(pos, np.float64)[:, None] * inv_freq[None, :]
    cos, sin = np.cos(ang), np.sin(ang)
    return (np.concatenate([cos, cos], axis=-1).astype(np.float32),
            np.concatenate([-sin, sin], axis=-1).astype(np.float32))


def _decay_tables(lb):
    log_gamma = np.log(1.0 - 2.0 ** (-5.0 - np.arange(RET_HEADS, dtype=np.float64)))
    idx = np.arange(lb, dtype=np.float64)
    diff = idx[:, None] - idx[None, :]
    dmask = np.where(diff >= 0, np.exp(log_gamma[:, None, None] * np.maximum(diff, 0.0)[None]), 0.0)
    qdec = np.exp(log_gamma[:, None] * (idx[None, :] + 1.0))
    kdec = np.exp(log_gamma[:, None] * (lb - 1.0 - idx[None, :]))
    sdec = np.exp(log_gamma * lb)
    bcast = lambda a: np.broadcast_to(a[..., None], a.shape + (HEAD_DIM,)).astype(np.float32)
    return [dmask.astype(np.float32), bcast(qdec), bcast(kdec), bcast(sdec)[:, None, :]]


def _branch_consts(weights, gblk, lb):
    g_ret, g_gmlp, w_s, b_s = weights
    bs_blk = jnp.broadcast_to(b_s[:, :gblk, None], (GMLP_GROUPS, gblk, HEAD_DIM))
    return [g_ret.reshape(1, -1), g_gmlp.reshape(1, -1), w_s[:, :gblk, :gblk], bs_blk] + _decay_tables(lb)


def _layer_prompt(x, mk16, mv16, g_norm, w_in16, w_out, weights, *, ts, lb):
    nb, l, _ = x.shape
    nt = l // ts
    n_out_slabs = D_MODEL // WOUT_SLAB
    assert n_out_slabs <= nb * nt
    tile = lambda width: pl.BlockSpec((1, ts, width), lambda b, t: (b, t, 0))
    per_stream = lambda shape: pl.BlockSpec(shape, lambda b, t: (b,) + (0,) * (len(shape) - 1))
    rope_spec = pl.BlockSpec((ts, HEAD_DIM), lambda b, t: (t, 0))
    out_slab_spec = pl.BlockSpec((MIX_WIDTH, WOUT_SLAB), lambda b, t: (0, jnp.minimum(b * nt + t, n_out_slabs - 1)))
    state_shape = (1, RET_HEADS, HEAD_DIM, HEAD_DIM)
    kv_spec = per_stream((1, MEM_LEN, XA_WIDTH))
    consts = [g_norm.reshape(1, -1), w_in16] + _branch_consts(weights, GMLP_CHUNK, lb)
    return pl.pallas_call(
        functools.partial(_prompt_kernel, lb=lb),
        grid=(nb, nt),
        in_specs=[tile(D_MODEL), rope_spec, rope_spec, kv_spec, kv_spec] + [_const_spec(c.shape) for c in consts]
        + [out_slab_spec],
        out_specs=[tile(MIX_WIDTH), per_stream(state_shape), out_slab_spec],
        out_shape=[jax.ShapeDtypeStruct((nb, l, MIX_WIDTH), _BF16),
                   jax.ShapeDtypeStruct((nb,) + state_shape[1:], _F32),
                   jax.ShapeDtypeStruct(w_out.shape, _BF16)],
        scratch_shapes=[pltpu.VMEM((ts, D_MODEL), _BF16)],
        compiler_params=pltpu.CompilerParams(
            dimension_semantics=("arbitrary", "arbitrary"), vmem_limit_bytes=_V7X_VMEM_LIMIT_BYTES),
        name="layer_prompt",
    )(x, *_rope_tables(np.arange(l)), mk16, mv16, *consts, w_out)


def _layer_sample(x, cache_k, cache_v, s0, g_norm, w_in, weights, *, sp):
    n, ts, _ = x.shape
    m = n * ts
    n_slabs = IN_WIDTH // W_SLAB
    branch_step = lambda i: jnp.maximum(i - n_slabs, 0)
    per_step = lambda shape: pl.BlockSpec(shape, lambda i: (branch_step(i),) + (0,) * (len(shape) - 1))
    slab_spec = pl.BlockSpec((D_MODEL, W_SLAB), lambda i: (0, jnp.minimum(i, n_slabs - 1)))
    state_shape = (sp, RET_HEADS, HEAD_DIM, HEAD_DIM)
    kv_spec = per_step((sp, XA_HEADS, MEM_LEN, HEAD_DIM))
    consts = _branch_consts(weights, ts, ts)
    w_in16, mix, s_new, gvn = pl.pallas_call(
        functools.partial(_sample_kernel, ts=ts, sp=sp, n_slabs=n_slabs),
        grid=(n_slabs + n // sp,),
        in_specs=[_const_spec((m, D_MODEL)), _const_spec((m, HEAD_DIM)), _const_spec((m, HEAD_DIM)), kv_spec, kv_spec,
                  per_step(state_shape), _const_spec((1, D_MODEL)), slab_spec]
        + [_const_spec(c.shape) for c in consts],
        out_specs=[slab_spec, pl.BlockSpec((m, MIX_WIDTH), lambda i: (0, 0)), per_step(state_shape),
                   pl.BlockSpec((m, GMLP_WIDTH), lambda i: (0, 0))],
        out_shape=[jax.ShapeDtypeStruct(w_in.shape, _BF16),
                   jax.ShapeDtypeStruct((m, MIX_WIDTH), _BF16),
                   jax.ShapeDtypeStruct((n,) + state_shape[1:], _F32),
                   jax.ShapeDtypeStruct((m, GMLP_WIDTH), _F32)],
        scratch_shapes=[pltpu.VMEM((m, D_MODEL), _BF16), pltpu.VMEM((n_slabs, m, W_SLAB), _F32)],
        compiler_params=pltpu.CompilerParams(
            dimension_semantics=("arbitrary",), vmem_limit_bytes=_V7X_VMEM_LIMIT_BYTES),
        name="layer_sample",
    )(x.reshape(m, D_MODEL), *_rope_tables(np.tile(PAST_LEN + np.arange(ts), n)), cache_k, cache_v, s0,
      g_norm.reshape(1, -1), w_in, *consts)
    return w_in16, mix, s_new, gvn.reshape(n, ts, GMLP_WIDTH)


def _out_proj(x_p, mix_p, x_s, mix_s, w_out16, g_final, *, tile_rows):
    n_p = x_p.shape[0] // tile_rows
    p_blk = pl.BlockSpec((tile_rows, D_MODEL), lambda n: (jnp.minimum(n, n_p - 1), 0))
    s_out_blk = pl.BlockSpec(x_s.shape, lambda n: (0, 0))
    return pl.pallas_call(
        functools.partial(_outproj_kernel, n_prompt_tiles=n_p),
        grid=(n_p + 1,),
        in_specs=[p_blk, p_blk, _const_spec(x_s.shape), _const_spec(mix_s.shape), _const_spec(w_out16.shape),
                  _const_spec((1, D_MODEL))],
        out_specs=[p_blk, s_out_blk],
        out_shape=[jax.ShapeDtypeStruct(x_p.shape, _F32), jax.ShapeDtypeStruct(x_s.shape, _F32)],
        compiler_params=pltpu.CompilerParams(
            dimension_semantics=("arbitrary",), vmem_limit_bytes=_V7X_VMEM_LIMIT_BYTES),
        name="out_proj",
    )(x_p, mix_p, x_s, mix_s, w_out16, g_final.reshape(1, -1))


def _memory_kv(mem, g_mem, w_mem_kv):
    b = mem.shape[0]
    blk = lambda width: pl.BlockSpec((1, MEM_LEN, width), lambda i: (i, 0, 0))
    blk4 = pl.BlockSpec((1, MEM_LEN, XA_HEADS, HEAD_DIM), lambda i: (i, 0, 0, 0))
    kv_f32 = jax.ShapeDtypeStruct((b, MEM_LEN, XA_HEADS, HEAD_DIM), _F32)
    kv_b16 = jax.ShapeDtypeStruct((b, MEM_LEN, XA_WIDTH), _BF16)
    return pl.pallas_call(
        _memkv_kernel,
        grid=(b,),
        in_specs=[blk(D_MODEL), _const_spec((1, D_MODEL)), _const_spec(w_mem_kv.shape)],
        out_specs=[blk4, blk4, blk(XA_WIDTH), blk(XA_WIDTH)],
        out_shape=[kv_f32, kv_f32, kv_b16, kv_b16],
        scratch_shapes=[pltpu.VMEM(w_mem_kv.shape, _BF16)],
        compiler_params=pltpu.CompilerParams(dimension_semantics=("arbitrary",)),
        name="memory_kv",
    )(mem, g_mem.reshape(1, -1), w_mem_kv)


PROMPT_TILE = 512
RET_BLOCK = 256
OUT_TILE = 512
SAMPLE_STEP = 4


def kernel(x_prompt, x_sample, mem_prompt, state_ret, cache_mem_k, cache_mem_v, g_norm, w_in, g_ret, g_gmlp,
           w_s, b_s, g_mem, w_mem_kv, w_out, g_final):
    assert g_norm.shape[0] == 1
    b_p, l_p, _ = x_prompt.shape
    b_s_, l_s, _ = x_sample.shape
    weights = (g_ret[0], g_gmlp[0], w_s[0], b_s[0])

    mk, mv, mk16, mv16 = _memory_kv(mem_prompt, g_mem[0], w_mem_kv[0])

    head_major16 = lambda c: jnp.transpose(c[0], (0, 2, 1, 3)).astype(_BF16)
    w_in16, mix_s, s_s, gvn_s = _layer_sample(
        x_sample, head_major16(cache_mem_k), head_major16(cache_mem_v), state_ret[0], g_norm[0], w_in[0], weights,
        sp=SAMPLE_STEP)

    mix_p, s_p, w_out16 = _layer_prompt(x_prompt, mk16, mv16, g_norm[0], w_in16, w_out[0], weights,
                                        ts=PROMPT_TILE, lb=RET_BLOCK)
    y_p, y_s = _out_proj(x_prompt.reshape(b_p * l_p, D_MODEL), mix_p.reshape(b_p * l_p, MIX_WIDTH),
                         x_sample.reshape(b_s_ * l_s, D_MODEL), mix_s, w_out16, g_final, tile_rows=OUT_TILE)

    return (y_p.reshape(x_prompt.shape), y_s.reshape(x_sample.shape), s_p[None], mk[None], mv[None],
            s_s[None], gvn_s[None])
```

```python
import functools

import numpy as np

import jax
import jax.numpy as jnp
from jax import lax
from jax.experimental import pallas as pl
from jax.experimental.pallas import tpu as pltpu

D_MODEL = 2048
PAST_LEN = 1024
MEM_LEN = 256
RET_HEADS = 8
HEAD_DIM = 128
RET_WIDTH = RET_HEADS * HEAD_DIM
GMLP_GROUPS = 4
GMLP_WIDTH = GMLP_GROUPS * HEAD_DIM
GMLP_CHUNK = 128
XA_HEADS = 4
XA_WIDTH = XA_HEADS * HEAD_DIM
MIX_WIDTH = RET_WIDTH + GMLP_WIDTH + XA_WIDTH
ROPE_BASE = 10000.0
EPS = 1e-6

_RQ, _RK, _RV, _RG = 0, RET_WIDTH, 2 * RET_WIDTH, 3 * RET_WIDTH
_GU = 4 * RET_WIDTH
_GV = _GU + GMLP_WIDTH
_GG = _GV + GMLP_WIDTH
_AQ = _GG + GMLP_WIDTH
_AG = _AQ + XA_WIDTH
IN_WIDTH = _AG + XA_WIDTH
W_SLAB = 4 * HEAD_DIM
WOUT_SLAB = HEAD_DIM
_MIX_GMLP = RET_WIDTH
_MIX_XA = RET_WIDTH + GMLP_WIDTH

_V7X_VMEM_LIMIT_BYTES = 60 * 1024 * 1024

_BF16 = jnp.bfloat16
_F32 = jnp.float32


def _dot(a, b):
    return jnp.dot(a, b, preferred_element_type=_F32)


def _dot_nt(a, b):
    return lax.dot_general(a, b, (((1,), (1,)), ((), ())), preferred_element_type=_F32)


def _dot_tn(a, b):
    return lax.dot_general(a, b, (((0,), (0,)), ((), ())), preferred_element_type=_F32)


def _silu(x):
    return x / (1.0 + jnp.exp(-x))


def _rms_scale(x):
    return x * lax.rsqrt(jnp.mean(x * x, axis=-1, keepdims=True) + EPS)


def _center_scale(x):
    mu = jnp.mean(x, axis=-1, keepdims=True)
    d = x - mu
    return d * lax.rsqrt(jnp.mean(d * d, axis=-1, keepdims=True) + EPS)


def _rope(u, cos2, sin2):
    return u * cos2 + pltpu.roll(u, HEAD_DIM // 2, 1) * sin2


def _cols(base, i):
    return slice(base + i * HEAD_DIM, base + (i + 1) * HEAD_DIM)


def _retention_blocks(blocks, dmask_ref, qdec_ref, kdec_ref, sdec_ref):
    sc = [_dot_nt(qb, kb.astype(_BF16)) for qb, kb, _, _, _ in blocks]
    cross = [_dot(qb, state.astype(_BF16)) for qb, _, _, state, _ in blocks]
    new_states = [state * sdec_ref[head] + _dot_tn((kb * kdec_ref[head]).astype(_BF16), vb)
                  for _, kb, vb, state, head in blocks]
    o = [_dot((s * dmask_ref[head]).astype(_BF16), vb) + c * qdec_ref[head]
         for s, c, (_, _, vb, _, head) in zip(sc, cross, blocks)]
    return o, new_states


def _tril_bf16(w):
    n = w.shape[0]
    row = lax.broadcasted_iota(jnp.int32, (n, n), 0)
    col = lax.broadcasted_iota(jnp.int32, (n, n), 1)
    return jnp.where(row >= col, w, 0.0).astype(_BF16)


def _xattn_heads(heads):
    sc = [_dot_nt(aq.astype(_BF16), mk) * (HEAD_DIM ** -0.5) for aq, mk, _ in heads]
    e = [jnp.exp(s - jnp.max(s, axis=-1, keepdims=True)) for s in sc]
    return [_dot(p.astype(_BF16), mv) / jnp.sum(p, axis=-1, keepdims=True) for p, (_, _, mv) in zip(e, heads)]


def _memkv_kernel(mem_ref, g_ref, w_ref, k_ref, v_ref, k16_ref, v16_ref, w16_s):
    @pl.when(pl.program_id(0) == 0)
    def _():
        w16_s[...] = w_ref[...].astype(_BF16)

    nb = mem_ref.shape[0]
    n = (_rms_scale(mem_ref[...].reshape(nb * MEM_LEN, D_MODEL)) * g_ref[...]).astype(_BF16)
    kv = _dot(n, w16_s[...])
    for i in range(nb):
        k = kv[i * MEM_LEN:(i + 1) * MEM_LEN, :XA_WIDTH]
        v = kv[i * MEM_LEN:(i + 1) * MEM_LEN, XA_WIDTH:]
        for hd in range(XA_HEADS):
            k_ref[i, :, hd, :] = k[:, _cols(0, hd)]
            v_ref[i, :, hd, :] = v[:, _cols(0, hd)]
        k16_ref[i] = k.astype(_BF16)
        v16_ref[i] = v.astype(_BF16)


def _prompt_kernel(x_ref, cos_ref, sin_ref, mk_ref, mv_ref, gnorm_ref, win_ref, gret_ref, ggm_ref, ws_ref, bs_ref,
                   dmask_ref, qdec_ref, kdec_ref, sdec_ref, wout_ref, mix_ref, s_ref, wout16_ref, h_s,
                   *, lb, n_out_slabs):
    ts = x_ref.shape[1]

    @pl.when(pl.program_id(0) * pl.num_programs(1) + pl.program_id(1) < n_out_slabs)
    def _():
        wout16_ref[...] = wout_ref[...].astype(_BF16)

    @pl.when(pl.program_id(1) == 0)
    def _():
        s_ref[...] = jnp.zeros_like(s_ref)

    h_s[...] = (_rms_scale(x_ref[0]) * gnorm_ref[...]).astype(_BF16)

    def proj(c0, width):
        return _dot(h_s[...], win_ref[:, c0:c0 + width])

    cos = cos_ref[...]
    sin = sin_ref[...]

    decay = (dmask_ref, qdec_ref, kdec_ref, sdec_ref)

    def retention_pair(p):
        c0 = 2 * HEAD_DIM * p
        q2, k2, v2, g2 = (proj(sec + c0, 2 * HEAD_DIM) for sec in (_RQ, _RK, _RV, _RG))
        heads = (2 * p, 2 * p + 1)
        qh = [_rope(q2[:, _cols(0, hh)], cos, sin).astype(_BF16) for hh in range(2)]
        kh = [_rope(k2[:, _cols(0, hh)], cos, sin) * (HEAD_DIM ** -0.5) for hh in range(2)]
        vh = [v2[:, _cols(0, hh)].astype(_BF16) for hh in range(2)]
        for bi in range(ts // lb):
            rs = slice(bi * lb, (bi + 1) * lb)
            o, new_states = _retention_blocks(
                [(qh[hh][rs], kh[hh][rs], vh[hh][rs], s_ref[0, head], head) for hh, head in enumerate(heads)],
                *decay)
            for hh, head in enumerate(heads):
                s_ref[0, head] = new_states[hh]
                on = _center_scale(o[hh]) * gret_ref[:, _cols(0, head)]
                mix_ref[0, rs, _cols(0, head)] = (on * _silu(g2[rs, _cols(0, hh)])).astype(_BF16)

    def gmlp():
        gvn = _center_scale(proj(_GV, GMLP_WIDTH)) * ggm_ref[...]
        gg = proj(_GG, GMLP_WIDTH)
        gu = proj(_GU, GMLP_WIDTH)
        for g in range(GMLP_GROUPS):
            hs = _cols(0, g)
            wm = _tril_bf16(ws_ref[g])
            for c in range(ts // GMLP_CHUNK):
                rs = slice(c * GMLP_CHUNK, (c + 1) * GMLP_CHUNK)
                sg = _dot(wm, gvn[rs, hs].astype(_BF16)) + bs_ref[g]
                mix_ref[0, rs, _cols(_MIX_GMLP, g)] = (gu[rs, hs] * (sg * _silu(gg[rs, hs]))).astype(_BF16)

    def xattn():
        aq = proj(_AQ, XA_WIDTH)
        ag = proj(_AG, XA_WIDTH)
        ao = _xattn_heads([(aq[:, _cols(0, hd)], mk_ref[0, :, _cols(0, hd)], mv_ref[0, :, _cols(0, hd)])
                           for hd in range(XA_HEADS)])
        for hd in range(XA_HEADS):
            mix_ref[0, :, _cols(_MIX_XA, hd)] = (ao[hd] * _silu(ag[:, _cols(0, hd)])).astype(_BF16)

    for p in range(RET_HEADS // 2):
        retention_pair(p)
    xattn()
    gmlp()


class _SlabView:
    def __init__(self, ref):
        self.ref = ref

    def _at(self, rows, base, i):
        col = base + i * HEAD_DIM
        return (col // W_SLAB, rows, slice(col % W_SLAB, col % W_SLAB + HEAD_DIM))

    def get(self, rows, base, i):
        return self.ref[self._at(rows, base, i)]

    def set(self, rows, base, i, value):
        self.ref[self._at(rows, base, i)] = value


def _sample_kernel(x_ref, cos_ref, sin_ref, mk_ref, mv_ref, s0_ref, gnorm_ref, w_ref, gret_ref, ggm_ref, ws_ref,
                   bs_ref, dmask_ref, qdec_ref, kdec_ref, sdec_ref, w16_ref, mix_ref, s_ref, gvn_ref, h_s, proj_s,
                   *, ts, sp, n_slabs):
    n = pl.program_id(0)
    proj = _SlabView(proj_s)
    all_rows = slice(None)

    @pl.when(n == 0)
    def _():
        h_s[...] = (_rms_scale(x_ref[...]) * gnorm_ref[...]).astype(_BF16)

    @pl.when(n < n_slabs)
    def _():
        w16 = w_ref[...].astype(_BF16)
        w16_ref[...] = w16
        proj_s[n] = _dot(h_s[...], w16)

    @pl.when(n == n_slabs - 1)
    def _():
        cos = cos_ref[...]
        sin = sin_ref[...]
        for head in range(RET_HEADS):
            proj.set(all_rows, _RQ, head, _rope(proj.get(all_rows, _RQ, head), cos, sin))
            proj.set(all_rows, _RK, head, _rope(proj.get(all_rows, _RK, head), cos, sin) * (HEAD_DIM ** -0.5))
        gv = jnp.concatenate([proj.get(all_rows, _GV, g) for g in range(GMLP_GROUPS)], axis=1)
        gvn = _center_scale(gv) * ggm_ref[...]
        gvn_ref[...] = gvn
        for g in range(GMLP_GROUPS):
            proj.set(all_rows, _GV, g, gvn[:, _cols(0, g)])

    @pl.when(n >= n_slabs)
    def _():
        j = n - n_slabs
        rows = [pl.ds(pl.multiple_of((j * sp + i) * ts, ts), ts) for i in range(sp)]

        units = [(i, head) for i in range(sp) for head in range(RET_HEADS)]
        o, new_states = _retention_blocks(
            [(proj.get(rows[i], _RQ, head).astype(_BF16), proj.get(rows[i], _RK, head),
              proj.get(rows[i], _RV, head).astype(_BF16), s0_ref[i, head], head) for i, head in units],
            dmask_ref, qdec_ref, kdec_ref, sdec_ref)
        for (i, head), o_u, s_u in zip(units, o, new_states):
            s_ref[i, head] = s_u
            on = _center_scale(o_u) * gret_ref[:, _cols(0, head)]
            mix_ref[rows[i], _cols(0, head)] = (on * _silu(proj.get(rows[i], _RG, head))).astype(_BF16)

        for g in range(GMLP_GROUPS):
            wm = _tril_bf16(ws_ref[g])
            for i in range(sp):
                sg = _dot(wm, proj.get(rows[i], _GV, g).astype(_BF16)) + bs_ref[g]
                mix_ref[rows[i], _cols(_MIX_GMLP, g)] = (
                    proj.get(rows[i], _GU, g) * sg * _silu(proj.get(rows[i], _GG, g))).astype(_BF16)

        units = [(i, hd) for i in range(sp) for hd in range(XA_HEADS)]
        ao = _xattn_heads([(proj.get(rows[i], _AQ, hd), mk_ref[i, hd], mv_ref[i, hd]) for i, hd in units])
        for (i, hd), ao_u in zip(units, ao):
            mix_ref[rows[i], _cols(_MIX_XA, hd)] = (ao_u * _silu(proj.get(rows[i], _AG, hd))).astype(_BF16)


def _outproj_kernel(xp_ref, mixp_ref, xs_ref, mixs_ref, w_ref, g_ref, yp_ref, ys_ref, *, n_prompt_tiles):
    n = pl.program_id(0)

    def project(x_ref, mix_ref, y_ref):
        y = x_ref[...] + _dot(mix_ref[...], w_ref[...])
        y_ref[...] = _rms_scale(y) * g_ref[...]

    @pl.when(n < n_prompt_tiles)
    def _():
        project(xp_ref, mixp_ref, yp_ref)

    @pl.when(n == n_prompt_tiles)
    def _():
        project(xs_ref, mixs_ref, ys_ref)


def _const_spec(shape):
    return pl.BlockSpec(shape, lambda *_: (0,) * len(shape), pipeline_mode=pl.Buffered(1))


def _rope_tables(pos):
    inv_freq = ROPE_BASE ** (-np.arange(0, HEAD_DIM, 2, dtype=np.float64) / HEAD_DIM)
    ang = np.asarray(pos, np.float64)[:, None] * inv_freq[None, :]
    cos, sin = np.cos(ang), np.sin(ang)
    return (np.concatenate([cos, cos], axis=-1).astype(np.float32),
            np.concatenate([-sin, sin], axis=-1).astype(np.float32))


def _decay_tables(lb):
    log_gamma = np.log(1.0 - 2.0 ** (-5.0 - np.arange(RET_HEADS, dtype=np.float64)))
    idx = np.arange(lb, dtype=np.float64)
    diff = idx[:, None] - idx[None, :]
    dmask = np.where(diff >= 0, np.exp(log_gamma[:, None, None] * np.maximum(diff, 0.0)[None]), 0.0)
    qdec = np.exp(log_gamma[:, None] * (idx[None, :] + 1.0))
    kdec = np.exp(log_gamma[:, None] * (lb - 1.0 - idx[None, :]))
    sdec = np.exp(log_gamma * lb)
    bcast = lambda a: np.broadcast_to(a[..., None], a.shape + (HEAD_DIM,)).astype(np.float32)
    return [dmask.astype(np.float32), bcast(qdec), bcast(kdec), bcast(sdec)[:, None, :]]


def _branch_consts(weights, gblk, lb):
    g_ret, g_gmlp, w_s, b_s = weights
    bs_blk = jnp.broadcast_to(b_s[:, :gblk, None], (GMLP_GROUPS, gblk, HEAD_DIM))
    return [g_ret.reshape(1, -1), g_gmlp.reshape(1, -1), w_s[:, :gblk, :gblk], bs_blk] + _decay_tables(lb)


def _layer_prompt(x, mk16, mv16, g_norm, w_in16, w_out, weights, *, ts, lb):
    nb, l, _ = x.shape
    nt = l // ts
    n_out_slabs = D_MODEL // WOUT_SLAB
    assert n_out_slabs <= nb * nt
    tile = lambda width: pl.BlockSpec((1, ts, width), lambda b, t: (b, t, 0))
    per_stream = lambda shape: pl.BlockSpec(shape, lambda b, t: (b,) + (0,) * (len(shape) - 1))
    rope_spec = pl.BlockSpec((ts, HEAD_DIM), lambda b, t: (t, 0))
    out_slab_spec = pl.BlockSpec((MIX_WIDTH, WOUT_SLAB), lambda b, t: (0, jnp.minimum(b * nt + t, n_out_slabs - 1)))
    state_shape = (1, RET_HEADS, HEAD_DIM, HEAD_DIM)
    kv_spec = per_stream((1, MEM_LEN, XA_WIDTH))
    consts = [g_norm.reshape(1, -1), w_in16] + _branch_consts(weights, GMLP_CHUNK, lb)
    return pl.pallas_call(
        functools.partial(_prompt_kernel, lb=lb, n_out_slabs=n_out_slabs),
        grid=(nb, nt),
        in_specs=[tile(D_MODEL), rope_spec, rope_spec, kv_spec, kv_spec] + [_const_spec(c.shape) for c in consts]
        + [out_slab_spec],
        out_specs=[tile(MIX_WIDTH), per_stream(state_shape), out_slab_spec],
        out_shape=[jax.ShapeDtypeStruct((nb, l, MIX_WIDTH), _BF16),
                   jax.ShapeDtypeStruct((nb,) + state_shape[1:], _F32),
                   jax.ShapeDtypeStruct(w_out.shape, _BF16)],
        scratch_shapes=[pltpu.VMEM((ts, D_MODEL), _BF16)],
        compiler_params=pltpu.CompilerParams(
            dimension_semantics=("arbitrary", "arbitrary"), vmem_limit_bytes=_V7X_VMEM_LIMIT_BYTES),
        name="layer_prompt",
    )(x, *_rope_tables(np.arange(l)), mk16, mv16, *consts, w_out)


def _layer_sample(x, cache_k, cache_v, s0, g_norm, w_in, weights, *, sp):
    n, ts, _ = x.shape
    m = n * ts
    n_slabs = IN_WIDTH // W_SLAB
    branch_step = lambda i: jnp.maximum(i - n_slabs, 0)
    per_step = lambda shape: pl.BlockSpec(shape, lambda i: (branch_step(i),) + (0,) * (len(shape) - 1))
    slab_spec = pl.BlockSpec((D_MODEL, W_SLAB), lambda i: (0, jnp.minimum(i, n_slabs - 1)))
    state_shape = (sp, RET_HEADS, HEAD_DIM, HEAD_DIM)
    kv_spec = per_step((sp, XA_HEADS, MEM_LEN, HEAD_DIM))
    consts = _branch_consts(weights, ts, ts)
    w_in16, mix, s_new, gvn = pl.pallas_call(
        functools.partial(_sample_kernel, ts=ts, sp=sp, n_slabs=n_slabs),
        grid=(n_slabs + n // sp,),
        in_specs=[_const_spec((m, D_MODEL)), _const_spec((m, HEAD_DIM)), _const_spec((m, HEAD_DIM)), kv_spec, kv_spec,
                  per_step(state_shape), _const_spec((1, D_MODEL)), slab_spec]
        + [_const_spec(c.shape) for c in consts],
        out_specs=[slab_spec, pl.BlockSpec((m, MIX_WIDTH), lambda i: (0, 0)), per_step(state_shape),
                   pl.BlockSpec((m, GMLP_WIDTH), lambda i: (0, 0))],
        out_shape=[jax.ShapeDtypeStruct(w_in.shape, _BF16),
                   jax.ShapeDtypeStruct((m, MIX_WIDTH), _BF16),
                   jax.ShapeDtypeStruct((n,) + state_shape[1:], _F32),
                   jax.ShapeDtypeStruct((m, GMLP_WIDTH), _F32)],
        scratch_shapes=[pltpu.VMEM((m, D_MODEL), _BF16), pltpu.VMEM((n_slabs, m, W_SLAB), _F32)],
        compiler_params=pltpu.CompilerParams(
            dimension_semantics=("arbitrary",), vmem_limit_bytes=_V7X_VMEM_LIMIT_BYTES),
        name="layer_sample",
    )(x.reshape(m, D_MODEL), *_rope_tables(np.tile(PAST_LEN + np.arange(ts), n)), cache_k, cache_v, s0,
      g_norm.reshape(1, -1), w_in, *consts)
    return w_in16, mix, s_new, gvn.reshape(n, ts, GMLP_WIDTH)


def _out_proj(x_p, mix_p, x_s, mix_s, w_out16, g_final, *, tile_rows):
    n_p = x_p.shape[0] // tile_rows
    p_blk = pl.BlockSpec((tile_rows, D_MODEL), lambda n: (jnp.minimum(n, n_p - 1), 0))
    s_out_blk = pl.BlockSpec(x_s.shape, lambda n: (0, 0))
    return pl.pallas_call(
        functools.partial(_outproj_kernel, n_prompt_tiles=n_p),
        grid=(n_p + 1,),
        in_specs=[p_blk, p_blk, _const_spec(x_s.shape), _const_spec(mix_s.shape), _const_spec(w_out16.shape),
                  _const_spec((1, D_MODEL))],
        out_specs=[p_blk, s_out_blk],
        out_shape=[jax.ShapeDtypeStruct(x_p.shape, _F32), jax.ShapeDtypeStruct(x_s.shape, _F32)],
        compiler_params=pltpu.CompilerParams(
            dimension_semantics=("arbitrary",), vmem_limit_bytes=_V7X_VMEM_LIMIT_BYTES),
        name="out_proj",
    )(x_p, mix_p, x_s, mix_s, w_out16, g_final.reshape(1, -1))


def _memory_kv(mem, g_mem, w_mem_kv):
    b = mem.shape[0]
    nb = MEMKV_STREAMS
    blk = lambda width: pl.BlockSpec((nb, MEM_LEN, width), lambda i: (i, 0, 0))
    blk4 = pl.BlockSpec((nb, MEM_LEN, XA_HEADS, HEAD_DIM), lambda i: (i, 0, 0, 0))
    kv_f32 = jax.ShapeDtypeStruct((b, MEM_LEN, XA_HEADS, HEAD_DIM), _F32)
    kv_b16 = jax.ShapeDtypeStruct((b, MEM_LEN, XA_WIDTH), _BF16)
    return pl.pallas_call(
        _memkv_kernel,
        grid=(b // nb,),
        in_specs=[blk(D_MODEL), _const_spec((1, D_MODEL)), _const_spec(w_mem_kv.shape)],
        out_specs=[blk4, blk4, blk(XA_WIDTH), blk(XA_WIDTH)],
        out_shape=[kv_f32, kv_f32, kv_b16, kv_b16],
        scratch_shapes=[pltpu.VMEM(w_mem_kv.shape, _BF16)],
        compiler_params=pltpu.CompilerParams(dimension_semantics=("arbitrary",)),
        name="memory_kv",
    )(mem, g_mem.reshape(1, -1), w_mem_kv)


MEMKV_STREAMS = 2
PROMPT_TILE = 512
RET_BLOCK = 256
OUT_TILE = 512
SAMPLE_STEP = 4


def kernel(x_prompt, x_sample, mem_prompt, state_ret, cache_mem_k, cache_mem_v, g_norm, w_in, g_ret, g_gmlp,
           w_s, b_s, g_mem, w_mem_kv, w_out, g_final):
    assert g_norm.shape[0] == 1
    b_p, l_p, _ = x_prompt.shape
    b_s_, l_s, _ = x_sample.shape
    weights = (g_ret[0], g_gmlp[0], w_s[0], b_s[0])

    mk, mv, mk16, mv16 = _memory_kv(mem_prompt, g_mem[0], w_mem_kv[0])

    head_major16 = lambda c: jnp.transpose(c[0], (0, 2, 1, 3)).astype(_BF16)
    w_in16, mix_s, s_s, gvn_s = _layer_sample(
        x_sample, head_major16(cache_mem_k), head_major16(cache_mem_v), state_ret[0], g_norm[0], w_in[0], weights,
        sp=SAMPLE_STEP)

    mix_p, s_p, w_out16 = _layer_prompt(x_prompt, mk16, mv16, g_norm[0], w_in16, w_out[0], weights,
                                        ts=PROMPT_TILE, lb=RET_BLOCK)
    y_p, y_s = _out_proj(x_prompt.reshape(b_p * l_p, D_MODEL), mix_p.reshape(b_p * l_p, MIX_WIDTH),
                         x_sample.reshape(b_s_ * l_s, D_MODEL), mix_s, w_out16, g_final, tile_rows=OUT_TILE)

    return (y_p.reshape(x_prompt.shape), y_s.reshape(x_sample.shape), s_p[None], mk[None], mv[None],
            s_s[None], gvn_s[None])
```

```python
import functools

import numpy as np

import jax
import jax.numpy as jnp
from jax import lax
from jax.experimental import pallas as pl
from jax.experimental.pallas import tpu as pltpu

D_MODEL = 2048
PAST_LEN = 1024
MEM_LEN = 256
RET_HEADS = 8
HEAD_DIM = 128
RET_WIDTH = RET_HEADS * HEAD_DIM
GMLP_GROUPS = 4
GMLP_WIDTH = GMLP_GROUPS * HEAD_DIM
GMLP_CHUNK = 128
XA_HEADS = 4
XA_WIDTH = XA_HEADS * HEAD_DIM
MIX_WIDTH = RET_WIDTH + GMLP_WIDTH + XA_WIDTH
ROPE_BASE = 10000.0
EPS = 1e-6

_RQ, _RK, _RV, _RG = 0, RET_WIDTH, 2 * RET_WIDTH, 3 * RET_WIDTH
_GU = 4 * RET_WIDTH
_GV = _GU + GMLP_WIDTH
_GG = _GV + GMLP_WIDTH
_AQ = _GG + GMLP_WIDTH
_AG = _AQ + XA_WIDTH
IN_WIDTH = _AG + XA_WIDTH
W_SLAB = 4 * HEAD_DIM
WOUT_SLAB = HEAD_DIM
_MIX_GMLP = RET_WIDTH
_MIX_XA = RET_WIDTH + GMLP_WIDTH

_V7X_VMEM_LIMIT_BYTES = 60 * 1024 * 1024

_BF16 = jnp.bfloat16
_F32 = jnp.float32


def _dot(a, b):
    return jnp.dot(a, b, preferred_element_type=_F32)


def _dot_nt(a, b):
    return lax.dot_general(a, b, (((1,), (1,)), ((), ())), preferred_element_type=_F32)


def _dot_tn(a, b):
    return lax.dot_general(a, b, (((0,), (0,)), ((), ())), preferred_element_type=_F32)


def _silu(x):
    return x / (1.0 + jnp.exp(-x))


def _rms_scale(x):
    return x * lax.rsqrt(jnp.mean(x * x, axis=-1, keepdims=True) + EPS)


def _center_scale(x):
    mu = jnp.mean(x, axis=-1, keepdims=True)
    d = x - mu
    return d * lax.rsqrt(jnp.mean(d * d, axis=-1, keepdims=True) + EPS)


def _rope(u, cos2, sin2):
    return u * cos2 + pltpu.roll(u, HEAD_DIM // 2, 1) * sin2


def _cols(base, i):
    return slice(base + i * HEAD_DIM, base + (i + 1) * HEAD_DIM)


def _retention_blocks(blocks, dmask_ref, qdec_ref, kdec_ref, sdec_ref):
    sc = [_dot_nt(qb, kb.astype(_BF16)) for qb, kb, _, _, _ in blocks]
    cross = [_dot(qb, state.astype(_BF16)) for qb, _, _, state, _ in blocks]
    new_states = [state * sdec_ref[head] + _dot_tn((kb * kdec_ref[head]).astype(_BF16), vb)
                  for _, kb, vb, state, head in blocks]
    o = [_dot((s * dmask_ref[head]).astype(_BF16), vb) + c * qdec_ref[head]
         for s, c, (_, _, vb, _, head) in zip(sc, cross, blocks)]
    return o, new_states


def _fill_row_bias(dst_ref, bs_ref, n):
    row = lax.broadcasted_iota(jnp.int32, (n, n), 0)
    col = lax.broadcasted_iota(jnp.int32, (n, n), 1)
    for g in range(GMLP_GROUPS):
        diag = jnp.where(row == col, jnp.broadcast_to(bs_ref[g:g + 1, :n], (n, n)), 0.0)
        dst_ref[g] = jnp.broadcast_to(jnp.sum(diag, axis=1, keepdims=True), (n, HEAD_DIM))


def _tril_bf16(w):
    n = w.shape[0]
    row = lax.broadcasted_iota(jnp.int32, (n, n), 0)
    col = lax.broadcasted_iota(jnp.int32, (n, n), 1)
    return jnp.where(row >= col, w, 0.0).astype(_BF16)


def _xattn_heads(heads):
    sc = [_dot_nt(aq.astype(_BF16), mk) * (HEAD_DIM ** -0.5) for aq, mk, _ in heads]
    e = [jnp.exp(s - jnp.max(s, axis=-1, keepdims=True)) for s in sc]
    return [_dot(p.astype(_BF16), mv) / jnp.sum(p, axis=-1, keepdims=True) for p, (_, _, mv) in zip(e, heads)]


def _memkv_kernel(mem_ref, g_ref, w_ref, k_ref, v_ref, k16_ref, v16_ref, w16_s):
    @pl.when(pl.program_id(0) == 0)
    def _():
        w16_s[...] = w_ref[...].astype(_BF16)

    nb = mem_ref.shape[0]
    n = (_rms_scale(mem_ref[...].reshape(nb * MEM_LEN, D_MODEL)) * g_ref[...]).astype(_BF16)
    kv = _dot(n, w16_s[...])
    for i in range(nb):
        k = kv[i * MEM_LEN:(i + 1) * MEM_LEN, :XA_WIDTH]
        v = kv[i * MEM_LEN:(i + 1) * MEM_LEN, XA_WIDTH:]
        for hd in range(XA_HEADS):
            k_ref[i, :, hd, :] = k[:, _cols(0, hd)]
            v_ref[i, :, hd, :] = v[:, _cols(0, hd)]
        k16_ref[i] = k.astype(_BF16)
        v16_ref[i] = v.astype(_BF16)


def _prompt_kernel(x_ref, cos_ref, sin_ref, mk_ref, mv_ref, gnorm_ref, win_ref, gret_ref, ggm_ref, ws_ref, bs_ref,
                   dmask_ref, qdec_ref, kdec_ref, sdec_ref, wout_ref, mix_ref, s_ref, wout16_ref, h_s, bias_s,
                   *, lb, n_out_slabs):
    ts = x_ref.shape[1]
    step = pl.program_id(0) * pl.num_programs(1) + pl.program_id(1)

    @pl.when(step == 0)
    def _():
        _fill_row_bias(bias_s, bs_ref, GMLP_CHUNK)

    @pl.when(step < n_out_slabs)
    def _():
        wout16_ref[...] = wout_ref[...].astype(_BF16)

    @pl.when(pl.program_id(1) == 0)
    def _():
        s_ref[...] = jnp.zeros_like(s_ref)

    h_s[...] = (_rms_scale(x_ref[0]) * gnorm_ref[...]).astype(_BF16)

    def proj(c0, width):
        return _dot(h_s[...], win_ref[:, c0:c0 + width])

    cos = cos_ref[...]
    sin = sin_ref[...]

    decay = (dmask_ref, qdec_ref, kdec_ref, sdec_ref)

    def retention_pair(p):
        c0 = 2 * HEAD_DIM * p
        q2, k2, v2, g2 = (proj(sec + c0, 2 * HEAD_DIM) for sec in (_RQ, _RK, _RV, _RG))
        heads = (2 * p, 2 * p + 1)
        qh = [_rope(q2[:, _cols(0, hh)], cos, sin).astype(_BF16) for hh in range(2)]
        kh = [_rope(k2[:, _cols(0, hh)], cos, sin) * (HEAD_DIM ** -0.5) for hh in range(2)]
        vh = [v2[:, _cols(0, hh)].astype(_BF16) for hh in range(2)]
        for bi in range(ts // lb):
            rs = slice(bi * lb, (bi + 1) * lb)
            o, new_states = _retention_blocks(
                [(qh[hh][rs], kh[hh][rs], vh[hh][rs], s_ref[0, head], head) for hh, head in enumerate(heads)],
                *decay)
            for hh, head in enumerate(heads):
                s_ref[0, head] = new_states[hh]
                on = _center_scale(o[hh]) * gret_ref[:, _cols(0, head)]
                mix_ref[0, rs, _cols(0, head)] = (on * _silu(g2[rs, _cols(0, hh)])).astype(_BF16)

    def gmlp():
        gvn = _center_scale(proj(_GV, GMLP_WIDTH)) * ggm_ref[...]
        gg = proj(_GG, GMLP_WIDTH)
        gu = proj(_GU, GMLP_WIDTH)
        for g in range(GMLP_GROUPS):
            hs = _cols(0, g)
            wm = _tril_bf16(ws_ref[g])
            for c in range(ts // GMLP_CHUNK):
                rs = slice(c * GMLP_CHUNK, (c + 1) * GMLP_CHUNK)
                sg = _dot(wm, gvn[rs, hs].astype(_BF16)) + bias_s[g]
                mix_ref[0, rs, _cols(_MIX_GMLP, g)] = (gu[rs, hs] * (sg * _silu(gg[rs, hs]))).astype(_BF16)

    def xattn():
        aq = proj(_AQ, XA_WIDTH)
        ag = proj(_AG, XA_WIDTH)
        ao = _xattn_heads([(aq[:, _cols(0, hd)], mk_ref[0, :, _cols(0, hd)], mv_ref[0, :, _cols(0, hd)])
                           for hd in range(XA_HEADS)])
        for hd in range(XA_HEADS):
            mix_ref[0, :, _cols(_MIX_XA, hd)] = (ao[hd] * _silu(ag[:, _cols(0, hd)])).astype(_BF16)

    for p in range(RET_HEADS // 2):
        retention_pair(p)
    xattn()
    gmlp()


class _SlabView:
    def __init__(self, ref):
        self.ref = ref

    def _at(self, rows, base, i):
        col = base + i * HEAD_DIM
        return (col // W_SLAB, rows, slice(col % W_SLAB, col % W_SLAB + HEAD_DIM))

    def get(self, rows, base, i):
        return self.ref[self._at(rows, base, i)]

    def set(self, rows, base, i, value):
        self.ref[self._at(rows, base, i)] = value


def _sample_kernel(x_ref, cos_ref, sin_ref, mk_ref, mv_ref, s0_ref, gnorm_ref, w_ref, gret_ref, ggm_ref, ws_ref,
                   bs_ref, dmask_ref, qdec_ref, kdec_ref, sdec_ref, w16_ref, mix_ref, s_ref, gvn_ref, h_s, proj_s,
                   bias_s, *, ts, sp, n_slabs):
    n = pl.program_id(0)
    proj = _SlabView(proj_s)
    all_rows = slice(None)

    @pl.when(n == 0)
    def _():
        h_s[...] = (_rms_scale(x_ref[...]) * gnorm_ref[...]).astype(_BF16)
        _fill_row_bias(bias_s, bs_ref, ts)

    @pl.when(n < n_slabs)
    def _():
        w16 = w_ref[...].astype(_BF16)
        w16_ref[...] = w16
        proj_s[n] = _dot(h_s[...], w16)

    @pl.when(n == n_slabs - 1)
    def _():
        cos = cos_ref[...]
        sin = sin_ref[...]
        for head in range(RET_HEADS):
            proj.set(all_rows, _RQ, head, _rope(proj.get(all_rows, _RQ, head), cos, sin))
            proj.set(all_rows, _RK, head, _rope(proj.get(all_rows, _RK, head), cos, sin) * (HEAD_DIM ** -0.5))
        gv = jnp.concatenate([proj.get(all_rows, _GV, g) for g in range(GMLP_GROUPS)], axis=1)
        gvn = _center_scale(gv) * ggm_ref[...]
        gvn_ref[...] = gvn
        for g in range(GMLP_GROUPS):
            proj.set(all_rows, _GV, g, gvn[:, _cols(0, g)])

    @pl.when(n >= n_slabs)
    def _():
        j = n - n_slabs
        rows = [pl.ds(pl.multiple_of((j * sp + i) * ts, ts), ts) for i in range(sp)]

        units = [(i, head) for i in range(sp) for head in range(RET_HEADS)]
        o, new_states = _retention_blocks(
            [(proj.get(rows[i], _RQ, head).astype(_BF16), proj.get(rows[i], _RK, head),
              proj.get(rows[i], _RV, head).astype(_BF16), s0_ref[i, head], head) for i, head in units],
            dmask_ref, qdec_ref, kdec_ref, sdec_ref)
        for (i, head), o_u, s_u in zip(units, o, new_states):
            s_ref[i, head] = s_u
            on = _center_scale(o_u) * gret_ref[:, _cols(0, head)]
            mix_ref[rows[i], _cols(0, head)] = (on * _silu(proj.get(rows[i], _RG, head))).astype(_BF16)

        for g in range(GMLP_GROUPS):
            wm = _tril_bf16(ws_ref[g, :ts, :ts])
            for i in range(sp):
                sg = _dot(wm, proj.get(rows[i], _GV, g).astype(_BF16)) + bias_s[g]
                mix_ref[rows[i], _cols(_MIX_GMLP, g)] = (
                    proj.get(rows[i], _GU, g) * sg * _silu(proj.get(rows[i], _GG, g))).astype(_BF16)

        units = [(i, hd) for i in range(sp) for hd in range(XA_HEADS)]
        ao = _xattn_heads([(proj.get(rows[i], _AQ, hd), mk_ref[i, hd], mv_ref[i, hd]) for i, hd in units])
        for (i, hd), ao_u in zip(units, ao):
            mix_ref[rows[i], _cols(_MIX_XA, hd)] = (ao_u * _silu(proj.get(rows[i], _AG, hd))).astype(_BF16)


def _outproj_kernel(xp_ref, mixp_ref, xs_ref, mixs_ref, w_ref, g_ref, yp_ref, ys_ref, *, n_prompt_tiles):
    n = pl.program_id(0)

    def project(x_ref, mix_ref, y_ref):
        y = x_ref[...] + _dot(mix_ref[...], w_ref[...])
        y_ref[...] = _rms_scale(y) * g_ref[...]

    @pl.when(n < n_prompt_tiles)
    def _():
        project(xp_ref, mixp_ref, yp_ref)

    @pl.when(n == n_prompt_tiles)
    def _():
        project(xs_ref, mixs_ref, ys_ref)


def _const_spec(shape):
    return pl.BlockSpec(shape, lambda *_: (0,) * len(shape), pipeline_mode=pl.Buffered(1))


def _rope_tables(pos):
    inv_freq = ROPE_BASE ** (-np.arange(0, HEAD_DIM, 2, dtype=np.float64) / HEAD_DIM)
    ang = np.asarray(pos, np.float64)[:, None] * inv_freq[None, :]
    cos, sin = np.cos(ang), np.sin(ang)
    return (np.concatenate([cos, cos], axis=-1).astype(np.float32),
            np.concatenate([-sin, sin], axis=-1).astype(np.float32))


def _decay_tables(lb):
    log_gamma = np.log(1.0 - 2.0 ** (-5.0 - np.arange(RET_HEADS, dtype=np.float64)))
    idx = np.arange(lb, dtype=np.float64)
    diff = idx[:, None] - idx[None, :]
    dmask = np.where(diff >= 0, np.exp(log_gamma[:, None, None] * np.maximum(diff, 0.0)[None]), 0.0)
    qdec = np.exp(log_gamma[:, None] * (idx[None, :] + 1.0))
    kdec = np.exp(log_gamma[:, None] * (lb - 1.0 - idx[None, :]))
    sdec = np.exp(log_gamma * lb)
    bcast = lambda a: np.broadcast_to(a[..., None], a.shape + (HEAD_DIM,)).astype(np.float32)
    return [dmask.astype(np.float32), bcast(qdec), bcast(kdec), bcast(sdec)[:, None, :]]


def _branch_consts(weights, lb):
    g_ret, g_gmlp, w_s, b_s = weights
    return [g_ret.reshape(1, -1), g_gmlp.reshape(1, -1), w_s, b_s] + _decay_tables(lb)


def _layer_prompt(x, mk16, mv16, g_norm, w_in16, w_out, weights, *, ts, lb):
    nb, l, _ = x.shape
    nt = l // ts
    n_out_slabs = D_MODEL // WOUT_SLAB
    assert n_out_slabs <= nb * nt
    tile = lambda width: pl.BlockSpec((1, ts, width), lambda b, t: (b, t, 0))
    per_stream = lambda shape: pl.BlockSpec(shape, lambda b, t: (b,) + (0,) * (len(shape) - 1))
    rope_spec = pl.BlockSpec((ts, HEAD_DIM), lambda b, t: (t, 0))
    out_slab_spec = pl.BlockSpec((MIX_WIDTH, WOUT_SLAB), lambda b, t: (0, jnp.minimum(b * nt + t, n_out_slabs - 1)))
    state_shape = (1, RET_HEADS, HEAD_DIM, HEAD_DIM)
    kv_spec = per_stream((1, MEM_LEN, XA_WIDTH))
    consts = [g_norm.reshape(1, -1), w_in16] + _branch_consts(weights, lb)
    return pl.pallas_call(
        functools.partial(_prompt_kernel, lb=lb, n_out_slabs=n_out_slabs),
        grid=(nb, nt),
        in_specs=[tile(D_MODEL), rope_spec, rope_spec, kv_spec, kv_spec] + [_const_spec(c.shape) for c in consts]
        + [out_slab_spec],
        out_specs=[tile(MIX_WIDTH), per_stream(state_shape), out_slab_spec],
        out_shape=[jax.ShapeDtypeStruct((nb, l, MIX_WIDTH), _BF16),
                   jax.ShapeDtypeStruct((nb,) + state_shape[1:], _F32),
                   jax.ShapeDtypeStruct(w_out.shape, _BF16)],
        scratch_shapes=[pltpu.VMEM((ts, D_MODEL), _BF16), pltpu.VMEM((GMLP_GROUPS, GMLP_CHUNK, HEAD_DIM), _F32)],
        compiler_params=pltpu.CompilerParams(
            dimension_semantics=("arbitrary", "arbitrary"), vmem_limit_bytes=_V7X_VMEM_LIMIT_BYTES),
        name="layer_prompt",
    )(x, *_rope_tables(np.arange(l)), mk16, mv16, *consts, w_out)


def _layer_sample(x, cache_k, cache_v, s0, g_norm, w_in, weights, *, sp):
    n, ts, _ = x.shape
    m = n * ts
    n_slabs = IN_WIDTH // W_SLAB
    branch_step = lambda i: jnp.maximum(i - n_slabs, 0)
    per_step = lambda shape: pl.BlockSpec(shape, lambda i: (branch_step(i),) + (0,) * (len(shape) - 1))
    slab_spec = pl.BlockSpec((D_MODEL, W_SLAB), lambda i: (0, jnp.minimum(i, n_slabs - 1)))
    state_shape = (sp, RET_HEADS, HEAD_DIM, HEAD_DIM)
    kv_spec = per_step((sp, XA_HEADS, MEM_LEN, HEAD_DIM))
    consts = _branch_consts(weights, ts)
    w_in16, mix, s_new, gvn = pl.pallas_call(
        functools.partial(_sample_kernel, ts=ts, sp=sp, n_slabs=n_slabs),
        grid=(n_slabs + n // sp,),
        in_specs=[_const_spec((m, D_MODEL)), _const_spec((m, HEAD_DIM)), _const_spec((m, HEAD_DIM)), kv_spec, kv_spec,
                  per_step(state_shape), _const_spec((1, D_MODEL)), slab_spec]
        + [_const_spec(c.shape) for c in consts],
        out_specs=[slab_spec, pl.BlockSpec((m, MIX_WIDTH), lambda i: (0, 0)), per_step(state_shape),
                   pl.BlockSpec((m, GMLP_WIDTH), lambda i: (0, 0))],
        out_shape=[jax.ShapeDtypeStruct(w_in.shape, _BF16),
                   jax.ShapeDtypeStruct((m, MIX_WIDTH), _BF16),
                   jax.ShapeDtypeStruct((n,) + state_shape[1:], _F32),
                   jax.ShapeDtypeStruct((m, GMLP_WIDTH), _F32)],
        scratch_shapes=[pltpu.VMEM((m, D_MODEL), _BF16), pltpu.VMEM((n_slabs, m, W_SLAB), _F32),
                        pltpu.VMEM((GMLP_GROUPS, ts, HEAD_DIM), _F32)],
        compiler_params=pltpu.CompilerParams(
            dimension_semantics=("arbitrary",), vmem_limit_bytes=_V7X_VMEM_LIMIT_BYTES),
        name="layer_sample",
    )(x.reshape(m, D_MODEL), *_rope_tables(np.tile(PAST_LEN + np.arange(ts), n)), cache_k, cache_v, s0,
      g_norm.reshape(1, -1), w_in, *consts)
    return w_in16, mix, s_new, gvn.reshape(n, ts, GMLP_WIDTH)


def _out_proj(x_p, mix_p, x_s, mix_s, w_out16, g_final, *, tile_rows):
    n_p = x_p.shape[0] // tile_rows
    p_blk = pl.BlockSpec((tile_rows, D_MODEL), lambda n: (jnp.minimum(n, n_p - 1), 0))
    s_out_blk = pl.BlockSpec(x_s.shape, lambda n: (0, 0))
    return pl.pallas_call(
        functools.partial(_outproj_kernel, n_prompt_tiles=n_p),
        grid=(n_p + 1,),
        in_specs=[p_blk, p_blk, _const_spec(x_s.shape), _const_spec(mix_s.shape), _const_spec(w_out16.shape),
                  _const_spec((1, D_MODEL))],
        out_specs=[p_blk, s_out_blk],
        out_shape=[jax.ShapeDtypeStruct(x_p.shape, _F32), jax.ShapeDtypeStruct(x_s.shape, _F32)],
        compiler_params=pltpu.CompilerParams(
            dimension_semantics=("arbitrary",), vmem_limit_bytes=_V7X_VMEM_LIMIT_BYTES),
        name="out_proj",
    )(x_p, mix_p, x_s, mix_s, w_out16, g_final.reshape(1, -1))


def _memory_kv(mem, g_mem, w_mem_kv):
    b = mem.shape[0]
    nb = MEMKV_STREAMS
    blk = lambda width: pl.BlockSpec((nb, MEM_LEN, width), lambda i: (i, 0, 0))
    blk4 = pl.BlockSpec((nb, MEM_LEN, XA_HEADS, HEAD_DIM), lambda i: (i, 0, 0, 0))
    kv_f32 = jax.ShapeDtypeStruct((b, MEM_LEN, XA_HEADS, HEAD_DIM), _F32)
    kv_b16 = jax.ShapeDtypeStruct((b, MEM_LEN, XA_WIDTH), _BF16)
    return pl.pallas_call(
        _memkv_kernel,
        grid=(b // nb,),
        in_specs=[blk(D_MODEL), _const_spec((1, D_MODEL)), _const_spec(w_mem_kv.shape)],
        out_specs=[blk4, blk4, blk(XA_WIDTH), blk(XA_WIDTH)],
        out_shape=[kv_f32, kv_f32, kv_b16, kv_b16],
        scratch_shapes=[pltpu.VMEM(w_mem_kv.shape, _BF16)],
        compiler_params=pltpu.CompilerParams(dimension_semantics=("arbitrary",)),
        name="memory_kv",
    )(mem, g_mem.reshape(1, -1), w_mem_kv)


MEMKV_STREAMS = 2
PROMPT_TILE = 512
RET_BLOCK = 256
OUT_TILE = 512
SAMPLE_STEP = 4


def kernel(x_prompt, x_sample, mem_prompt, state_ret, cache_mem_k, cache_mem_v, g_norm, w_in, g_ret, g_gmlp,
           w_s, b_s, g_mem, w_mem_kv, w_out, g_final):
    assert g_norm.shape[0] == 1
    b_p, l_p, _ = x_prompt.shape
    b_s_, l_s, _ = x_sample.shape
    weights = (g_ret[0], g_gmlp[0], w_s[0], b_s[0])

    mk, mv, mk16, mv16 = _memory_kv(mem_prompt, g_mem[0], w_mem_kv[0])

    head_major16 = lambda c: jnp.transpose(c[0], (0, 2, 1, 3)).astype(_BF16)
    w_in16, mix_s, s_s, gvn_s = _layer_sample(
        x_sample, head_major16(cache_mem_k), head_major16(cache_mem_v), state_ret[0], g_norm[0], w_in[0], weights,
        sp=SAMPLE_STEP)

    mix_p, s_p, w_out16 = _layer_prompt(x_prompt, mk16, mv16, g_norm[0], w_in16, w_out[0], weights,
                                        ts=PROMPT_TILE, lb=RET_BLOCK)
    y_p, y_s = _out_proj(x_prompt.reshape(b_p * l_p, D_MODEL), mix_p.reshape(b_p * l_p, MIX_WIDTH),
                         x_sample.reshape(b_s_ * l_s, D_MODEL), mix_s, w_out16, g_final, tile_rows=OUT_TILE)

    return (y_p.reshape(x_prompt.shape), y_s.reshape(x_sample.shape), s_p[None], mk[None], mv[None],
            s_s[None], gvn_s[None])
```

```python
import functools

import numpy as np

import jax
import jax.numpy as jnp
from jax import lax
from jax.experimental import pallas as pl
from jax.experimental.pallas import tpu as pltpu

D_MODEL = 2048
PAST_LEN = 1024
MEM_LEN = 256
RET_HEADS = 8
HEAD_DIM = 128
RET_WIDTH = RET_HEADS * HEAD_DIM
GMLP_GROUPS = 4
GMLP_WIDTH = GMLP_GROUPS * HEAD_DIM
GMLP_CHUNK = 128
XA_HEADS = 4
XA_WIDTH = XA_HEADS * HEAD_DIM
MIX_WIDTH = RET_WIDTH + GMLP_WIDTH + XA_WIDTH
ROPE_BASE = 10000.0
EPS = 1e-6

_RQ, _RK, _RV, _RG = 0, RET_WIDTH, 2 * RET_WIDTH, 3 * RET_WIDTH
_GU = 4 * RET_WIDTH
_GV = _GU + GMLP_WIDTH
_GG = _GV + GMLP_WIDTH
_AQ = _GG + GMLP_WIDTH
_AG = _AQ + XA_WIDTH
IN_WIDTH = _AG + XA_WIDTH
W_SLAB = 4 * HEAD_DIM
WOUT_SLAB = HEAD_DIM
_MIX_GMLP = RET_WIDTH
_MIX_XA = RET_WIDTH + GMLP_WIDTH

_V7X_VMEM_LIMIT_BYTES = 60 * 1024 * 1024

_BF16 = jnp.bfloat16
_F32 = jnp.float32


def _dot(a, b):
    return jnp.dot(a, b, preferred_element_type=_F32)


def _dot_nt(a, b):
    return lax.dot_general(a, b, (((1,), (1,)), ((), ())), preferred_element_type=_F32)


def _dot_tn(a, b):
    return lax.dot_general(a, b, (((0,), (0,)), ((), ())), preferred_element_type=_F32)


def _silu(x):
    return x / (1.0 + jnp.exp(-x))


def _rms_scale(x):
    return x * lax.rsqrt(jnp.mean(x * x, axis=-1, keepdims=True) + EPS)


def _center_scale(x):
    mu = jnp.mean(x, axis=-1, keepdims=True)
    d = x - mu
    return d * lax.rsqrt(jnp.mean(d * d, axis=-1, keepdims=True) + EPS)


def _rope(u, cos2, sin2):
    return u * cos2 + pltpu.roll(u, HEAD_DIM // 2, 1) * sin2


def _cols(base, i):
    return slice(base + i * HEAD_DIM, base + (i + 1) * HEAD_DIM)


def _retention_blocks(blocks, dmask_ref, qdec_ref, kdec_ref, sdec_ref):
    sc = [_dot_nt(qb, kb.astype(_BF16)) for qb, kb, _, _, _ in blocks]
    cross = [_dot(qb, state.astype(_BF16)) for qb, _, _, state, _ in blocks]
    new_states = [state * sdec_ref[head] + _dot_tn((kb * kdec_ref[head]).astype(_BF16), vb)
                  for _, kb, vb, state, head in blocks]
    o = [_dot((s * dmask_ref[head]).astype(_BF16), vb) + c * qdec_ref[head]
         for s, c, (_, _, vb, _, head) in zip(sc, cross, blocks)]
    return o, new_states


def _fill_row_bias(dst_ref, bs_ref, n):
    row = lax.broadcasted_iota(jnp.int32, (n, n), 0)
    col = lax.broadcasted_iota(jnp.int32, (n, n), 1)
    for g in range(GMLP_GROUPS):
        diag = jnp.where(row == col, jnp.broadcast_to(bs_ref[g:g + 1, :n], (n, n)), 0.0)
        dst_ref[g] = jnp.broadcast_to(jnp.sum(diag, axis=1, keepdims=True), (n, HEAD_DIM))


def _tril_bf16(w):
    n = w.shape[0]
    row = lax.broadcasted_iota(jnp.int32, (n, n), 0)
    col = lax.broadcasted_iota(jnp.int32, (n, n), 1)
    return jnp.where(row >= col, w, 0.0).astype(_BF16)


def _xattn_heads(heads):
    sc = [_dot_nt(aq.astype(_BF16), mk) * (HEAD_DIM ** -0.5) for aq, mk, _ in heads]
    e = [jnp.exp(s - jnp.max(s, axis=-1, keepdims=True)) for s in sc]
    return [_dot(p.astype(_BF16), mv) / jnp.sum(p, axis=-1, keepdims=True) for p, (_, _, mv) in zip(e, heads)]


def _memkv_kernel(mem_ref, g_ref, w_ref, k_ref, v_ref, k16_ref, v16_ref, w16_s):
    @pl.when(pl.program_id(0) == 0)
    def _():
        w16_s[...] = w_ref[...].astype(_BF16)

    nb = mem_ref.shape[0]
    n = (_rms_scale(mem_ref[...].reshape(nb * MEM_LEN, D_MODEL)) * g_ref[...]).astype(_BF16)
    kv = _dot(n, w16_s[...])
    for i in range(nb):
        k = kv[i * MEM_LEN:(i + 1) * MEM_LEN, :XA_WIDTH]
        v = kv[i * MEM_LEN:(i + 1) * MEM_LEN, XA_WIDTH:]
        for hd in range(XA_HEADS):
            k_ref[i, :, hd, :] = k[:, _cols(0, hd)]
            v_ref[i, :, hd, :] = v[:, _cols(0, hd)]
        k16_ref[i] = k.astype(_BF16)
        v16_ref[i] = v.astype(_BF16)


def _prompt_kernel(x_ref, cos_ref, sin_ref, mk_ref, mv_ref, gnorm_ref, win_ref, gret_ref, ggm_ref, ws_ref, bs_ref,
                   dmask_ref, qdec_ref, kdec_ref, sdec_ref, wout_ref, mix_ref, s_ref, wout16_ref, h_s, bias_s,
                   *, lb, n_out_slabs):
    ts = x_ref.shape[1]
    step = pl.program_id(0) * pl.num_programs(1) + pl.program_id(1)

    @pl.when(step == 0)
    def _():
        _fill_row_bias(bias_s, bs_ref, GMLP_CHUNK)

    @pl.when(step < n_out_slabs)
    def _():
        wout16_ref[...] = wout_ref[...].astype(_BF16)

    @pl.when(pl.program_id(1) == 0)
    def _():
        s_ref[...] = jnp.zeros_like(s_ref)

    x = x_ref[0]
    h_s[...] = (x * gnorm_ref[...]).astype(_BF16)
    row_scale = lax.rsqrt(jnp.mean(x * x, axis=-1, keepdims=True) + EPS)
    row_scale_b = jnp.broadcast_to(row_scale, (ts, HEAD_DIM))

    def proj(c0, width, scaled_rows=True):
        raw = _dot(h_s[...], win_ref[:, c0:c0 + width])
        if scaled_rows:
            return raw
        return jnp.concatenate([raw[:, _cols(0, i)] * row_scale_b for i in range(width // HEAD_DIM)], axis=1)

    cos = cos_ref[...]
    sin = sin_ref[...]

    decay = (dmask_ref, qdec_ref, kdec_ref, sdec_ref)

    def retention_pair(p):
        c0 = 2 * HEAD_DIM * p
        q2, k2, v2, g2 = (proj(sec + c0, 2 * HEAD_DIM, scaled_rows=p > 0) for sec in (_RQ, _RK, _RV, _RG))
        if p == 0:
            h_s[...] = (x_ref[0] * row_scale * gnorm_ref[...]).astype(_BF16)
        heads = (2 * p, 2 * p + 1)
        qh = [_rope(q2[:, _cols(0, hh)], cos, sin).astype(_BF16) for hh in range(2)]
        kh = [_rope(k2[:, _cols(0, hh)], cos, sin) * (HEAD_DIM ** -0.5) for hh in range(2)]
        vh = [v2[:, _cols(0, hh)].astype(_BF16) for hh in range(2)]
        for bi in range(ts // lb):
            rs = slice(bi * lb, (bi + 1) * lb)
            o, new_states = _retention_blocks(
                [(qh[hh][rs], kh[hh][rs], vh[hh][rs], s_ref[0, head], head) for hh, head in enumerate(heads)],
                *decay)
            for hh, head in enumerate(heads):
                s_ref[0, head] = new_states[hh]
                on = _center_scale(o[hh]) * gret_ref[:, _cols(0, head)]
                mix_ref[0, rs, _cols(0, head)] = (on * _silu(g2[rs, _cols(0, hh)])).astype(_BF16)

    def gmlp():
        gvn = _center_scale(proj(_GV, GMLP_WIDTH)) * ggm_ref[...]
        gg = proj(_GG, GMLP_WIDTH)
        gu = proj(_GU, GMLP_WIDTH)
        for g in range(GMLP_GROUPS):
            hs = _cols(0, g)
            wm = _tril_bf16(ws_ref[g])
            for c in range(ts // GMLP_CHUNK):
                rs = slice(c * GMLP_CHUNK, (c + 1) * GMLP_CHUNK)
                sg = _dot(wm, gvn[rs, hs].astype(_BF16)) + bias_s[g]
                mix_ref[0, rs, _cols(_MIX_GMLP, g)] = (gu[rs, hs] * (sg * _silu(gg[rs, hs]))).astype(_BF16)

    def xattn():
        aq = proj(_AQ, XA_WIDTH)
        ag = proj(_AG, XA_WIDTH)
        ao = _xattn_heads([(aq[:, _cols(0, hd)], mk_ref[0, :, _cols(0, hd)], mv_ref[0, :, _cols(0, hd)])
                           for hd in range(XA_HEADS)])
        for hd in range(XA_HEADS):
            mix_ref[0, :, _cols(_MIX_XA, hd)] = (ao[hd] * _silu(ag[:, _cols(0, hd)])).astype(_BF16)

    for p in range(RET_HEADS // 2):
        retention_pair(p)
    xattn()
    gmlp()


class _SlabView:
    def __init__(self, ref):
        self.ref = ref

    def _at(self, rows, base, i):
        col = base + i * HEAD_DIM
        return (col // W_SLAB, rows, slice(col % W_SLAB, col % W_SLAB + HEAD_DIM))

    def get(self, rows, base, i):
        return self.ref[self._at(rows, base, i)]

    def set(self, rows, base, i, value):
        self.ref[self._at(rows, base, i)] = value


def _sample_kernel(x_ref, cos_ref, sin_ref, mk_ref, mv_ref, s0_ref, gnorm_ref, w_ref, gret_ref, ggm_ref, ws_ref,
                   bs_ref, dmask_ref, qdec_ref, kdec_ref, sdec_ref, w16_ref, mix_ref, s_ref, gvn_ref, h_s, proj_s,
                   bias_s, *, ts, sp, n_slabs):
    n = pl.program_id(0)
    proj = _SlabView(proj_s)
    all_rows = slice(None)

    @pl.when(n == 0)
    def _():
        h_s[...] = (_rms_scale(x_ref[...]) * gnorm_ref[...]).astype(_BF16)
        _fill_row_bias(bias_s, bs_ref, ts)

    @pl.when(n < n_slabs)
    def _():
        w16 = w_ref[...].astype(_BF16)
        w16_ref[...] = w16
        proj_s[n] = _dot(h_s[...], w16)

    @pl.when(n == n_slabs - 1)
    def _():
        cos = cos_ref[...]
        sin = sin_ref[...]
        for head in range(RET_HEADS):
            proj.set(all_rows, _RQ, head, _rope(proj.get(all_rows, _RQ, head), cos, sin))
            proj.set(all_rows, _RK, head, _rope(proj.get(all_rows, _RK, head), cos, sin) * (HEAD_DIM ** -0.5))
        gv = jnp.concatenate([proj.get(all_rows, _GV, g) for g in range(GMLP_GROUPS)], axis=1)
        gvn = _center_scale(gv) * ggm_ref[...]
        gvn_ref[...] = gvn
        for g in range(GMLP_GROUPS):
            proj.set(all_rows, _GV, g, gvn[:, _cols(0, g)])

    @pl.when(n >= n_slabs)
    def _():
        j = n - n_slabs
        rows = [pl.ds(pl.multiple_of((j * sp + i) * ts, ts), ts) for i in range(sp)]

        units = [(i, head) for i in range(sp) for head in range(RET_HEADS)]
        o, new_states = _retention_blocks(
            [(proj.get(rows[i], _RQ, head).astype(_BF16), proj.get(rows[i], _RK, head),
              proj.get(rows[i], _RV, head).astype(_BF16), s0_ref[i, head], head) for i, head in units],
            dmask_ref, qdec_ref, kdec_ref, sdec_ref)
        for (i, head), o_u, s_u in zip(units, o, new_states):
            s_ref[i, head] = s_u
            on = _center_scale(o_u) * gret_ref[:, _cols(0, head)]
            mix_ref[rows[i], _cols(0, head)] = (on * _silu(proj.get(rows[i], _RG, head))).astype(_BF16)

        for g in range(GMLP_GROUPS):
            wm = _tril_bf16(ws_ref[g, :ts, :ts])
            for i in range(sp):
                sg = _dot(wm, proj.get(rows[i], _GV, g).astype(_BF16)) + bias_s[g]
                mix_ref[rows[i], _cols(_MIX_GMLP, g)] = (
                    proj.get(rows[i], _GU, g) * sg * _silu(proj.get(rows[i], _GG, g))).astype(_BF16)

        units = [(i, hd) for i in range(sp) for hd in range(XA_HEADS)]
        ao = _xattn_heads([(proj.get(rows[i], _AQ, hd), mk_ref[i, hd], mv_ref[i, hd]) for i, hd in units])
        for (i, hd), ao_u in zip(units, ao):
            mix_ref[rows[i], _cols(_MIX_XA, hd)] = (ao_u * _silu(proj.get(rows[i], _AG, hd))).astype(_BF16)


def _outproj_kernel(xp_ref, mixp_ref, xs_ref, mixs_ref, w_ref, g_ref, yp_ref, ys_ref, *, n_prompt_tiles):
    n = pl.program_id(0)

    def project(x_ref, mix_ref, y_ref):
        y = x_ref[...] + _dot(mix_ref[...], w_ref[...])
        y_ref[...] = _rms_scale(y) * g_ref[...]

    @pl.when(n < n_prompt_tiles)
    def _():
        project(xp_ref, mixp_ref, yp_ref)

    @pl.when(n == n_prompt_tiles)
    def _():
        project(xs_ref, mixs_ref, ys_ref)


def _const_spec(shape):
    return pl.BlockSpec(shape, lambda *_: (0,) * len(shape), pipeline_mode=pl.Buffered(1))


def _rope_tables(pos):
    inv_freq = ROPE_BASE ** (-np.arange(0, HEAD_DIM, 2, dtype=np.float64) / HEAD_DIM)
    ang = np.asarray(pos, np.float64)[:, None] * inv_freq[None, :]
    cos, sin = np.cos(ang), np.sin(ang)
    return (np.concatenate([cos, cos], axis=-1).astype(np.float32),
            np.concatenate([-sin, sin], axis=-1).astype(np.float32))


def _decay_tables(lb):
    log_gamma = np.log(1.0 - 2.0 ** (-5.0 - np.arange(RET_HEADS, dtype=np.float64)))
    idx = np.arange(lb, dtype=np.float64)
    diff = idx[:, None] - idx[None, :]
    dmask = np.where(diff >= 0, np.exp(log_gamma[:, None, None] * np.maximum(diff, 0.0)[None]), 0.0)
    qdec = np.exp(log_gamma[:, None] * (idx[None, :] + 1.0))
    kdec = np.exp(log_gamma[:, None] * (lb - 1.0 - idx[None, :]))
    sdec = np.exp(log_gamma * lb)
    bcast = lambda a: np.broadcast_to(a[..., None], a.shape + (HEAD_DIM,)).astype(np.float32)
    return [dmask.astype(np.float32), bcast(qdec), bcast(kdec), bcast(sdec)[:, None, :]]


def _branch_consts(weights, lb):
    g_ret, g_gmlp, w_s, b_s = weights
    return [g_ret.reshape(1, -1), g_gmlp.reshape(1, -1), w_s, b_s] + _decay_tables(lb)


def _layer_prompt(x, mk16, mv16, g_norm, w_in16, w_out, weights, *, ts, lb):
    nb, l, _ = x.shape
    nt = l // ts
    n_out_slabs = D_MODEL // WOUT_SLAB
    assert n_out_slabs <= nb * nt
    tile = lambda width: pl.BlockSpec((1, ts, width), lambda b, t: (b, t, 0))
    per_stream = lambda shape: pl.BlockSpec(shape, lambda b, t: (b,) + (0,) * (len(shape) - 1))
    rope_spec = pl.BlockSpec((ts, HEAD_DIM), lambda b, t: (t, 0))
    out_slab_spec = pl.BlockSpec((MIX_WIDTH, WOUT_SLAB), lambda b, t: (0, jnp.minimum(b * nt + t, n_out_slabs - 1)))
    state_shape = (1, RET_HEADS, HEAD_DIM, HEAD_DIM)
    kv_spec = per_stream((1, MEM_LEN, XA_WIDTH))
    consts = [g_norm.reshape(1, -1), w_in16] + _branch_consts(weights, lb)
    return pl.pallas_call(
        functools.partial(_prompt_kernel, lb=lb, n_out_slabs=n_out_slabs),
        grid=(nb, nt),
        in_specs=[tile(D_MODEL), rope_spec, rope_spec, kv_spec, kv_spec] + [_const_spec(c.shape) for c in consts]
        + [out_slab_spec],
        out_specs=[tile(MIX_WIDTH), per_stream(state_shape), out_slab_spec],
        out_shape=[jax.ShapeDtypeStruct((nb, l, MIX_WIDTH), _BF16),
                   jax.ShapeDtypeStruct((nb,) + state_shape[1:], _F32),
                   jax.ShapeDtypeStruct(w_out.shape, _BF16)],
        scratch_shapes=[pltpu.VMEM((ts, D_MODEL), _BF16), pltpu.VMEM((GMLP_GROUPS, GMLP_CHUNK, HEAD_DIM), _F32)],
        compiler_params=pltpu.CompilerParams(
            dimension_semantics=("arbitrary", "arbitrary"), vmem_limit_bytes=_V7X_VMEM_LIMIT_BYTES),
        name="layer_prompt",
    )(x, *_rope_tables(np.arange(l)), mk16, mv16, *consts, w_out)


def _layer_sample(x, cache_k, cache_v, s0, g_norm, w_in, weights, *, sp):
    n, ts, _ = x.shape
    m = n * ts
    n_slabs = IN_WIDTH // W_SLAB
    branch_step = lambda i: jnp.maximum(i - n_slabs, 0)
    per_step = lambda shape: pl.BlockSpec(shape, lambda i: (branch_step(i),) + (0,) * (len(shape) - 1))
    slab_spec = pl.BlockSpec((D_MODEL, W_SLAB), lambda i: (0, jnp.minimum(i, n_slabs - 1)))
    state_shape = (sp, RET_HEADS, HEAD_DIM, HEAD_DIM)
    kv_spec = per_step((sp, XA_HEADS, MEM_LEN, HEAD_DIM))
    consts = _branch_consts(weights, ts)
    w_in16, mix, s_new, gvn = pl.pallas_call(
        functools.partial(_sample_kernel, ts=ts, sp=sp, n_slabs=n_slabs),
        grid=(n_slabs + n // sp,),
        in_specs=[_const_spec((m, D_MODEL)), _const_spec((m, HEAD_DIM)), _const_spec((m, HEAD_DIM)), kv_spec, kv_spec,
                  per_step(state_shape), _const_spec((1, D_MODEL)), slab_spec]
        + [_const_spec(c.shape) for c in consts],
        out_specs=[slab_spec, pl.BlockSpec((m, MIX_WIDTH), lambda i: (0, 0)), per_step(state_shape),
                   pl.BlockSpec((m, GMLP_WIDTH), lambda i: (0, 0))],
        out_shape=[jax.ShapeDtypeStruct(w_in.shape, _BF16),
                   jax.ShapeDtypeStruct((m, MIX_WIDTH), _BF16),
                   jax.ShapeDtypeStruct((n,) + state_shape[1:], _F32),
                   jax.ShapeDtypeStruct((m, GMLP_WIDTH), _F32)],
        scratch_shapes=[pltpu.VMEM((m, D_MODEL), _BF16), pltpu.VMEM((n_slabs, m, W_SLAB), _F32),
                        pltpu.VMEM((GMLP_GROUPS, ts, HEAD_DIM), _F32)],
        compiler_params=pltpu.CompilerParams(
            dimension_semantics=("arbitrary",), vmem_limit_bytes=_V7X_VMEM_LIMIT_BYTES),
        name="layer_sample",
    )(x.reshape(m, D_MODEL), *_rope_tables(np.tile(PAST_LEN + np.arange(ts), n)), cache_k, cache_v, s0,
      g_norm.reshape(1, -1), w_in, *consts)
    return w_in16, mix, s_new, gvn.reshape(n, ts, GMLP_WIDTH)


def _out_proj(x_p, mix_p, x_s, mix_s, w_out16, g_final, *, tile_rows):
    n_p = x_p.shape[0] // tile_rows
    p_blk = pl.BlockSpec((tile_rows, D_MODEL), lambda n: (jnp.minimum(n, n_p - 1), 0))
    s_out_blk = pl.BlockSpec(x_s.shape, lambda n: (0, 0))
    return pl.pallas_call(
        functools.partial(_outproj_kernel, n_prompt_tiles=n_p),
        grid=(n_p + 1,),
        in_specs=[p_blk, p_blk, _const_spec(x_s.shape), _const_spec(mix_s.shape), _const_spec(w_out16.shape),
                  _const_spec((1, D_MODEL))],
        out_specs=[p_blk, s_out_blk],
        out_shape=[jax.ShapeDtypeStruct(x_p.shape, _F32), jax.ShapeDtypeStruct(x_s.shape, _F32)],
        compiler_params=pltpu.CompilerParams(
            dimension_semantics=("arbitrary",), vmem_limit_bytes=_V7X_VMEM_LIMIT_BYTES),
        name="out_proj",
    )(x_p, mix_p, x_s, mix_s, w_out16, g_final.reshape(1, -1))


def _memory_kv(mem, g_mem, w_mem_kv):
    b = mem.shape[0]
    nb = MEMKV_STREAMS
    blk = lambda width: pl.BlockSpec((nb, MEM_LEN, width), lambda i: (i, 0, 0))
    blk4 = pl.BlockSpec((nb, MEM_LEN, XA_HEADS, HEAD_DIM), lambda i: (i, 0, 0, 0))
    kv_f32 = jax.ShapeDtypeStruct((b, MEM_LEN, XA_HEADS, HEAD_DIM), _F32)
    kv_b16 = jax.ShapeDtypeStruct((b, MEM_LEN, XA_WIDTH), _BF16)
    return pl.pallas_call(
        _memkv_kernel,
        grid=(b // nb,),
        in_specs=[blk(D_MODEL), _const_spec((1, D_MODEL)), _const_spec(w_mem_kv.shape)],
        out_specs=[blk4, blk4, blk(XA_WIDTH), blk(XA_WIDTH)],
        out_shape=[kv_f32, kv_f32, kv_b16, kv_b16],
        scratch_shapes=[pltpu.VMEM(w_mem_kv.shape, _BF16)],
        compiler_params=pltpu.CompilerParams(dimension_semantics=("arbitrary",)),
        name="memory_kv",
    )(mem, g_mem.reshape(1, -1), w_mem_kv)


MEMKV_STREAMS = 2
PROMPT_TILE = 512
RET_BLOCK = 256
OUT_TILE = 512
SAMPLE_STEP = 4


def kernel(x_prompt, x_sample, mem_prompt, state_ret, cache_mem_k, cache_mem_v, g_norm, w_in, g_ret, g_gmlp,
           w_s, b_s, g_mem, w_mem_kv, w_out, g_final):
    assert g_norm.shape[0] == 1
    b_p, l_p, _ = x_prompt.shape
    b_s_, l_s, _ = x_sample.shape
    weights = (g_ret[0], g_gmlp[0], w_s[0], b_s[0])

    mk, mv, mk16, mv16 = _memory_kv(mem_prompt, g_mem[0], w_mem_kv[0])

    head_major16 = lambda c: jnp.transpose(c[0], (0, 2, 1, 3)).astype(_BF16)
    w_in16, mix_s, s_s, gvn_s = _layer_sample(
        x_sample, head_major16(cache_mem_k), head_major16(cache_mem_v), state_ret[0], g_norm[0], w_in[0], weights,
        sp=SAMPLE_STEP)

    mix_p, s_p, w_out16 = _layer_prompt(x_prompt, mk16, mv16, g_norm[0], w_in16, w_out[0], weights,
                                        ts=PROMPT_TILE, lb=RET_BLOCK)
    y_p, y_s = _out_proj(x_prompt.reshape(b_p * l_p, D_MODEL), mix_p.reshape(b_p * l_p, MIX_WIDTH),
                         x_sample.reshape(b_s_ * l_s, D_MODEL), mix_s, w_out16, g_final, tile_rows=OUT_TILE)

    return (y_p.reshape(x_prompt.shape), y_s.reshape(x_sample.shape), s_p[None], mk[None], mv[None],
            s_s[None], gvn_s[None])
```

```python
import functools

import numpy as np

import jax
import jax.numpy as jnp
from jax import lax
from jax.experimental import pallas as pl
from jax.experimental.pallas import tpu as pltpu

D_MODEL = 2048
PAST_LEN = 1024
MEM_LEN = 256
RET_HEADS = 8
HEAD_DIM = 128
RET_WIDTH = RET_HEADS * HEAD_DIM
GMLP_GROUPS = 4
GMLP_WIDTH = GMLP_GROUPS * HEAD_DIM
GMLP_CHUNK = 128
XA_HEADS = 4
XA_WIDTH = XA_HEADS * HEAD_DIM
MIX_WIDTH = RET_WIDTH + GMLP_WIDTH + XA_WIDTH
ROPE_BASE = 10000.0
EPS = 1e-6

_RQ, _RK, _RV, _RG = 0, RET_WIDTH, 2 * RET_WIDTH, 3 * RET_WIDTH
_GU = 4 * RET_WIDTH
_GV = _GU + GMLP_WIDTH
_GG = _GV + GMLP_WIDTH
_AQ = _GG + GMLP_WIDTH
_AG = _AQ + XA_WIDTH
IN_WIDTH = _AG + XA_WIDTH
W_SLAB = 4 * HEAD_DIM
WOUT_SLAB = HEAD_DIM
_MIX_GMLP = RET_WIDTH
_MIX_XA = RET_WIDTH + GMLP_WIDTH

_V7X_VMEM_LIMIT_BYTES = 60 * 1024 * 1024

_BF16 = jnp.bfloat16
_F32 = jnp.float32


def _dot(a, b):
    return jnp.dot(a, b, preferred_element_type=_F32)


def _dot_nt(a, b):
    return lax.dot_general(a, b, (((1,), (1,)), ((), ())), preferred_element_type=_F32)


def _dot_tn(a, b):
    return lax.dot_general(a, b, (((0,), (0,)), ((), ())), preferred_element_type=_F32)


def _silu(x):
    return x / (1.0 + jnp.exp(-x))


def _rms_scale(x):
    return x * lax.rsqrt(jnp.mean(x * x, axis=-1, keepdims=True) + EPS)


def _center_scale(x):
    mu = jnp.mean(x, axis=-1, keepdims=True)
    d = x - mu
    return d * lax.rsqrt(jnp.mean(d * d, axis=-1, keepdims=True) + EPS)


def _rope(u, cos2, sin2):
    return u * cos2 + pltpu.roll(u, HEAD_DIM // 2, 1) * sin2


def _cols(base, i):
    return slice(base + i * HEAD_DIM, base + (i + 1) * HEAD_DIM)


def _retention_blocks(blocks, dmask_ref, qdec_ref, kdec_ref, sdec_ref):
    sc = [_dot_nt(qb, kb.astype(_BF16)) for qb, kb, _, _, _ in blocks]
    cross = [_dot(qb, state.astype(_BF16)) for qb, _, _, state, _ in blocks]
    new_states = [state * sdec_ref[head] + _dot_tn((kb * kdec_ref[head]).astype(_BF16), vb)
                  for _, kb, vb, state, head in blocks]
    o = [_dot((s * dmask_ref[head]).astype(_BF16), vb) + c * qdec_ref[head]
         for s, c, (_, _, vb, _, head) in zip(sc, cross, blocks)]
    return o, new_states


def _fill_row_bias(dst_ref, bs_ref, n):
    row = lax.broadcasted_iota(jnp.int32, (n, n), 0)
    col = lax.broadcasted_iota(jnp.int32, (n, n), 1)
    for g in range(GMLP_GROUPS):
        diag = jnp.where(row == col, jnp.broadcast_to(bs_ref[g:g + 1, :n], (n, n)), 0.0)
        dst_ref[g] = jnp.broadcast_to(jnp.sum(diag, axis=1, keepdims=True), (n, HEAD_DIM))


def _tril_bf16(w):
    n = w.shape[0]
    row = lax.broadcasted_iota(jnp.int32, (n, n), 0)
    col = lax.broadcasted_iota(jnp.int32, (n, n), 1)
    return jnp.where(row >= col, w, 0.0).astype(_BF16)


def _xattn_heads(heads):
    sc = [_dot_nt(aq.astype(_BF16), mk) * (HEAD_DIM ** -0.5) for aq, mk, _ in heads]
    e = [jnp.exp(s - jnp.max(s, axis=-1, keepdims=True)) for s in sc]
    return [_dot(p.astype(_BF16), mv) / jnp.sum(p, axis=-1, keepdims=True) for p, (_, _, mv) in zip(e, heads)]


def _memkv_kernel(mem_ref, g_ref, w_ref, k_ref, v_ref, k16_ref, v16_ref, w16_s):
    @pl.when(pl.program_id(0) == 0)
    def _():
        w16_s[...] = w_ref[...].astype(_BF16)

    nb = mem_ref.shape[0]
    n = (_rms_scale(mem_ref[...].reshape(nb * MEM_LEN, D_MODEL)) * g_ref[...]).astype(_BF16)
    kv = _dot(n, w16_s[...])
    for i in range(nb):
        k = kv[i * MEM_LEN:(i + 1) * MEM_LEN, :XA_WIDTH]
        v = kv[i * MEM_LEN:(i + 1) * MEM_LEN, XA_WIDTH:]
        for hd in range(XA_HEADS):
            k_ref[i, :, hd, :] = k[:, _cols(0, hd)]
            v_ref[i, :, hd, :] = v[:, _cols(0, hd)]
        k16_ref[i] = k.astype(_BF16)
        v16_ref[i] = v.astype(_BF16)


def _prompt_kernel(x_ref, cos_ref, sin_ref, mk_ref, mv_ref, gnorm_ref, win_ref, gret_ref, ggm_ref, ws_ref, bs_ref,
                   dmask_ref, qdec_ref, kdec_ref, sdec_ref, wout_ref, mix_ref, s_ref, wout16_ref, h_s, bias_s,
                   *, lb, n_out_slabs):
    ts = x_ref.shape[1]
    step = pl.program_id(0) * pl.num_programs(1) + pl.program_id(1)

    @pl.when(step == 0)
    def _():
        _fill_row_bias(bias_s, bs_ref, GMLP_CHUNK)

    @pl.when(step < n_out_slabs)
    def _():
        wout16_ref[...] = wout_ref[...].astype(_BF16)

    @pl.when(pl.program_id(1) == 0)
    def _():
        s_ref[...] = jnp.zeros_like(s_ref)

    x = x_ref[0]
    h_s[...] = (x * gnorm_ref[...]).astype(_BF16)
    row_scale = lax.rsqrt(jnp.mean(x * x, axis=-1, keepdims=True) + EPS)
    row_scale_b = jnp.broadcast_to(row_scale, (ts, HEAD_DIM))

    def proj(c0, width, scaled_rows=True):
        raw = _dot(h_s[...], win_ref[:, c0:c0 + width])
        if scaled_rows:
            return raw
        return jnp.concatenate([jnp.where(row_scale_b > 0.0, raw[:, _cols(0, i)] * row_scale_b, 0.0)
                                for i in range(width // HEAD_DIM)], axis=1)

    cos = cos_ref[...]
    sin = sin_ref[...]

    decay = (dmask_ref, qdec_ref, kdec_ref, sdec_ref)

    def retention_pair(p):
        c0 = 2 * HEAD_DIM * p
        q2, k2, v2, g2 = (proj(sec + c0, 2 * HEAD_DIM, scaled_rows=p > 0) for sec in (_RQ, _RK, _RV, _RG))
        if p == 0:
            h_s[...] = (x_ref[0] * row_scale * gnorm_ref[...]).astype(_BF16)
        heads = (2 * p, 2 * p + 1)
        qh = [_rope(q2[:, _cols(0, hh)], cos, sin).astype(_BF16) for hh in range(2)]
        kh = [_rope(k2[:, _cols(0, hh)], cos, sin) * (HEAD_DIM ** -0.5) for hh in range(2)]
        vh = [v2[:, _cols(0, hh)].astype(_BF16) for hh in range(2)]
        for bi in range(ts // lb):
            rs = slice(bi * lb, (bi + 1) * lb)
            o, new_states = _retention_blocks(
                [(qh[hh][rs], kh[hh][rs], vh[hh][rs], s_ref[0, head], head) for hh, head in enumerate(heads)],
                *decay)
            for hh, head in enumerate(heads):
                s_ref[0, head] = new_states[hh]
                on = _center_scale(o[hh]) * gret_ref[:, _cols(0, head)]
                mix_ref[0, rs, _cols(0, head)] = (on * _silu(g2[rs, _cols(0, hh)])).astype(_BF16)

    def gmlp():
        gvn = _center_scale(proj(_GV, GMLP_WIDTH)) * ggm_ref[...]
        gg = proj(_GG, GMLP_WIDTH)
        gu = proj(_GU, GMLP_WIDTH)
        for g in range(GMLP_GROUPS):
            hs = _cols(0, g)
            wm = _tril_bf16(ws_ref[g])
            for c in range(ts // GMLP_CHUNK):
                rs = slice(c * GMLP_CHUNK, (c + 1) * GMLP_CHUNK)
                sg = _dot(wm, gvn[rs, hs].astype(_BF16)) + bias_s[g]
                mix_ref[0, rs, _cols(_MIX_GMLP, g)] = (gu[rs, hs] * (sg * _silu(gg[rs, hs]))).astype(_BF16)

    def xattn():
        aq = proj(_AQ, XA_WIDTH)
        ag = proj(_AG, XA_WIDTH)
        ao = _xattn_heads([(aq[:, _cols(0, hd)], mk_ref[0, :, _cols(0, hd)], mv_ref[0, :, _cols(0, hd)])
                           for hd in range(XA_HEADS)])
        for hd in range(XA_HEADS):
            mix_ref[0, :, _cols(_MIX_XA, hd)] = (ao[hd] * _silu(ag[:, _cols(0, hd)])).astype(_BF16)

    for p in range(RET_HEADS // 2):
        retention_pair(p)
    xattn()
    gmlp()


class _SlabView:
    def __init__(self, ref):
        self.ref = ref

    def _at(self, rows, base, i):
        col = base + i * HEAD_DIM
        return (col // W_SLAB, rows, slice(col % W_SLAB, col % W_SLAB + HEAD_DIM))

    def get(self, rows, base, i):
        return self.ref[self._at(rows, base, i)]

    def set(self, rows, base, i, value):
        self.ref[self._at(rows, base, i)] = value


def _sample_kernel(x_ref, cos_ref, sin_ref, mk_ref, mv_ref, s0_ref, gnorm_ref, w_ref, gret_ref, ggm_ref, ws_ref,
                   bs_ref, dmask_ref, qdec_ref, kdec_ref, sdec_ref, w16_ref, mix_ref, s_ref, gvn_ref, h_s, proj_s,
                   bias_s, *, ts, sp, n_slabs):
    n = pl.program_id(0)
    proj = _SlabView(proj_s)
    all_rows = slice(None)

    @pl.when(n == 0)
    def _():
        h_s[...] = (_rms_scale(x_ref[...]) * gnorm_ref[...]).astype(_BF16)
        _fill_row_bias(bias_s, bs_ref, ts)

    @pl.when(n < n_slabs)
    def _():
        w16 = w_ref[...].astype(_BF16)
        w16_ref[...] = w16
        proj_s[n] = _dot(h_s[...], w16)

    @pl.when(n == n_slabs - 1)
    def _():
        cos = cos_ref[...]
        sin = sin_ref[...]
        for head in range(RET_HEADS):
            proj.set(all_rows, _RQ, head, _rope(proj.get(all_rows, _RQ, head), cos, sin))
            proj.set(all_rows, _RK, head, _rope(proj.get(all_rows, _RK, head), cos, sin) * (HEAD_DIM ** -0.5))
        gv = jnp.concatenate([proj.get(all_rows, _GV, g) for g in range(GMLP_GROUPS)], axis=1)
        gvn = _center_scale(gv) * ggm_ref[...]
        gvn_ref[...] = gvn
        for g in range(GMLP_GROUPS):
            proj.set(all_rows, _GV, g, gvn[:, _cols(0, g)])

    @pl.when(n >= n_slabs)
    def _():
        j = n - n_slabs
        rows = [pl.ds(pl.multiple_of((j * sp + i) * ts, ts), ts) for i in range(sp)]

        units = [(i, head) for i in range(sp) for head in range(RET_HEADS)]
        o, new_states = _retention_blocks(
            [(proj.get(rows[i], _RQ, head).astype(_BF16), proj.get(rows[i], _RK, head),
              proj.get(rows[i], _RV, head).astype(_BF16), s0_ref[i, head], head) for i, head in units],
            dmask_ref, qdec_ref, kdec_ref, sdec_ref)
        for (i, head), o_u, s_u in zip(units, o, new_states):
            s_ref[i, head] = s_u
            on = _center_scale(o_u) * gret_ref[:, _cols(0, head)]
            mix_ref[rows[i], _cols(0, head)] = (on * _silu(proj.get(rows[i], _RG, head))).astype(_BF16)

        for g in range(GMLP_GROUPS):
            wm = _tril_bf16(ws_ref[g, :ts, :ts])
            for i in range(sp):
                sg = _dot(wm, proj.get(rows[i], _GV, g).astype(_BF16)) + bias_s[g]
                mix_ref[rows[i], _cols(_MIX_GMLP, g)] = (
                    proj.get(rows[i], _GU, g) * sg * _silu(proj.get(rows[i], _GG, g))).astype(_BF16)

        units = [(i, hd) for i in range(sp) for hd in range(XA_HEADS)]
        ao = _xattn_heads([(proj.get(rows[i], _AQ, hd), mk_ref[i, hd], mv_ref[i, hd]) for i, hd in units])
        for (i, hd), ao_u in zip(units, ao):
            mix_ref[rows[i], _cols(_MIX_XA, hd)] = (ao_u * _silu(proj.get(rows[i], _AG, hd))).astype(_BF16)


def _outproj_kernel(xp_ref, mixp_ref, xs_ref, mixs_ref, w_ref, g_ref, yp_ref, ys_ref, *, n_prompt_tiles):
    n = pl.program_id(0)

    def project(x_ref, mix_ref, y_ref):
        y = x_ref[...] + _dot(mix_ref[...], w_ref[...])
        y_ref[...] = _rms_scale(y) * g_ref[...]

    @pl.when(n < n_prompt_tiles)
    def _():
        project(xp_ref, mixp_ref, yp_ref)

    @pl.when(n == n_prompt_tiles)
    def _():
        project(xs_ref, mixs_ref, ys_ref)


def _const_spec(shape):
    return pl.BlockSpec(shape, lambda *_: (0,) * len(shape), pipeline_mode=pl.Buffered(1))


def _rope_tables(pos):
    inv_freq = ROPE_BASE ** (-np.arange(0, HEAD_DIM, 2, dtype=np.float64) / HEAD_DIM)
    ang = np.asarray(pos, np.float64)[:, None] * inv_freq[None, :]
    cos, sin = np.cos(ang), np.sin(ang)
    return (np.concatenate([cos, cos], axis=-1).astype(np.float32),
            np.concatenate([-sin, sin], axis=-1).astype(np.float32))


def _decay_tables(lb):
    log_gamma = np.log(1.0 - 2.0 ** (-5.0 - np.arange(RET_HEADS, dtype=np.float64)))
    idx = np.arange(lb, dtype=np.float64)
    diff = idx[:, None] - idx[None, :]
    dmask = np.where(diff >= 0, np.exp(log_gamma[:, None, None] * np.maximum(diff, 0.0)[None]), 0.0)
    qdec = np.exp(log_gamma[:, None] * (idx[None, :] + 1.0))
    kdec = np.exp(log_gamma[:, None] * (lb - 1.0 - idx[None, :]))
    sdec = np.exp(log_gamma * lb)
    bcast = lambda a: np.broadcast_to(a[..., None], a.shape + (HEAD_DIM,)).astype(np.float32)
    return [dmask.astype(np.float32), bcast(qdec), bcast(kdec), bcast(sdec)[:, None, :]]


def _branch_consts(weights, lb):
    g_ret, g_gmlp, w_s, b_s = weights
    return [g_ret.reshape(1, -1), g_gmlp.reshape(1, -1), w_s, b_s] + _decay_tables(lb)


def _layer_prompt(x, mk16, mv16, g_norm, w_in16, w_out, weights, *, ts, lb):
    nb, l, _ = x.shape
    nt = l // ts
    n_out_slabs = D_MODEL // WOUT_SLAB
    assert n_out_slabs <= nb * nt
    tile = lambda width: pl.BlockSpec((1, ts, width), lambda b, t: (b, t, 0))
    per_stream = lambda shape: pl.BlockSpec(shape, lambda b, t: (b,) + (0,) * (len(shape) - 1))
    rope_spec = pl.BlockSpec((ts, HEAD_DIM), lambda b, t: (t, 0))
    out_slab_spec = pl.BlockSpec((MIX_WIDTH, WOUT_SLAB), lambda b, t: (0, jnp.minimum(b * nt + t, n_out_slabs - 1)))
    state_shape = (1, RET_HEADS, HEAD_DIM, HEAD_DIM)
    kv_spec = per_stream((1, MEM_LEN, XA_WIDTH))
    consts = [g_norm.reshape(1, -1), w_in16] + _branch_consts(weights, lb)
    return pl.pallas_call(
        functools.partial(_prompt_kernel, lb=lb, n_out_slabs=n_out_slabs),
        grid=(nb, nt),
        in_specs=[tile(D_MODEL), rope_spec, rope_spec, kv_spec, kv_spec] + [_const_spec(c.shape) for c in consts]
        + [out_slab_spec],
        out_specs=[tile(MIX_WIDTH), per_stream(state_shape), out_slab_spec],
        out_shape=[jax.ShapeDtypeStruct((nb, l, MIX_WIDTH), _BF16),
                   jax.ShapeDtypeStruct((nb,) + state_shape[1:], _F32),
                   jax.ShapeDtypeStruct(w_out.shape, _BF16)],
        scratch_shapes=[pltpu.VMEM((ts, D_MODEL), _BF16), pltpu.VMEM((GMLP_GROUPS, GMLP_CHUNK, HEAD_DIM), _F32)],
        compiler_params=pltpu.CompilerParams(
            dimension_semantics=("arbitrary", "arbitrary"), vmem_limit_bytes=_V7X_VMEM_LIMIT_BYTES),
        name="layer_prompt",
    )(x, *_rope_tables(np.arange(l)), mk16, mv16, *consts, w_out)


def _layer_sample(x, cache_k, cache_v, s0, g_norm, w_in, weights, *, sp):
    n, ts, _ = x.shape
    m = n * ts
    n_slabs = IN_WIDTH // W_SLAB
    branch_step = lambda i: jnp.maximum(i - n_slabs, 0)
    per_step = lambda shape: pl.BlockSpec(shape, lambda i: (branch_step(i),) + (0,) * (len(shape) - 1))
    slab_spec = pl.BlockSpec((D_MODEL, W_SLAB), lambda i: (0, jnp.minimum(i, n_slabs - 1)))
    state_shape = (sp, RET_HEADS, HEAD_DIM, HEAD_DIM)
    kv_spec = per_step((sp, XA_HEADS, MEM_LEN, HEAD_DIM))
    consts = _branch_consts(weights, ts)
    w_in16, mix, s_new, gvn = pl.pallas_call(
        functools.partial(_sample_kernel, ts=ts, sp=sp, n_slabs=n_slabs),
        grid=(n_slabs + n // sp,),
        in_specs=[_const_spec((m, D_MODEL)), _const_spec((m, HEAD_DIM)), _const_spec((m, HEAD_DIM)), kv_spec, kv_spec,
                  per_step(state_shape), _const_spec((1, D_MODEL)), slab_spec]
        + [_const_spec(c.shape) for c in consts],
        out_specs=[slab_spec, pl.BlockSpec((m, MIX_WIDTH), lambda i: (0, 0)), per_step(state_shape),
                   pl.BlockSpec((m, GMLP_WIDTH), lambda i: (0, 0))],
        out_shape=[jax.ShapeDtypeStruct(w_in.shape, _BF16),
                   jax.ShapeDtypeStruct((m, MIX_WIDTH), _BF16),
                   jax.ShapeDtypeStruct((n,) + state_shape[1:], _F32),
                   jax.ShapeDtypeStruct((m, GMLP_WIDTH), _F32)],
        scratch_shapes=[pltpu.VMEM((m, D_MODEL), _BF16), pltpu.VMEM((n_slabs, m, W_SLAB), _F32),
                        pltpu.VMEM((GMLP_GROUPS, ts, HEAD_DIM), _F32)],
        compiler_params=pltpu.CompilerParams(
            dimension_semantics=("arbitrary",), vmem_limit_bytes=_V7X_VMEM_LIMIT_BYTES),
        name="layer_sample",
    )(x.reshape(m, D_MODEL), *_rope_tables(np.tile(PAST_LEN + np.arange(ts), n)), cache_k, cache_v, s0,
      g_norm.reshape(1, -1), w_in, *consts)
    return w_in16, mix, s_new, gvn.reshape(n, ts, GMLP_WIDTH)


def _out_proj(x_p, mix_p, x_s, mix_s, w_out16, g_final, *, tile_rows):
    n_p = x_p.shape[0] // tile_rows
    p_blk = pl.BlockSpec((tile_rows, D_MODEL), lambda n: (jnp.minimum(n, n_p - 1), 0))
    s_out_blk = pl.BlockSpec(x_s.shape, lambda n: (0, 0))
    return pl.pallas_call(
        functools.partial(_outproj_kernel, n_prompt_tiles=n_p),
        grid=(n_p + 1,),
        in_specs=[p_blk, p_blk, _const_spec(x_s.shape), _const_spec(mix_s.shape), _const_spec(w_out16.shape),
                  _const_spec((1, D_MODEL))],
        out_specs=[p_blk, s_out_blk],
        out_shape=[jax.ShapeDtypeStruct(x_p.shape, _F32), jax.ShapeDtypeStruct(x_s.shape, _F32)],
        compiler_params=pltpu.CompilerParams(
            dimension_semantics=("arbitrary",), vmem_limit_bytes=_V7X_VMEM_LIMIT_BYTES),
        name="out_proj",
    )(x_p, mix_p, x_s, mix_s, w_out16, g_final.reshape(1, -1))


def _memory_kv(mem, g_mem, w_mem_kv):
    b = mem.shape[0]
    nb = MEMKV_STREAMS
    blk = lambda width: pl.BlockSpec((nb, MEM_LEN, width), lambda i: (i, 0, 0))
    blk4 = pl.BlockSpec((nb, MEM_LEN, XA_HEADS, HEAD_DIM), lambda i: (i, 0, 0, 0))
    kv_f32 = jax.ShapeDtypeStruct((b, MEM_LEN, XA_HEADS, HEAD_DIM), _F32)
    kv_b16 = jax.ShapeDtypeStruct((b, MEM_LEN, XA_WIDTH), _BF16)
    return pl.pallas_call(
        _memkv_kernel,
        grid=(b // nb,),
        in_specs=[blk(D_MODEL), _const_spec((1, D_MODEL)), _const_spec(w_mem_kv.shape)],
        out_specs=[blk4, blk4, blk(XA_WIDTH), blk(XA_WIDTH)],
        out_shape=[kv_f32, kv_f32, kv_b16, kv_b16],
        scratch_shapes=[pltpu.VMEM(w_mem_kv.shape, _BF16)],
        compiler_params=pltpu.CompilerParams(dimension_semantics=("arbitrary",)),
        name="memory_kv",
    )(mem, g_mem.reshape(1, -1), w_mem_kv)


MEMKV_STREAMS = 2
PROMPT_TILE = 512
RET_BLOCK = 256
OUT_TILE = 512
SAMPLE_STEP = 4


def kernel(x_prompt, x_sample, mem_prompt, state_ret, cache_mem_k, cache_mem_v, g_norm, w_in, g_ret, g_gmlp,
           w_s, b_s, g_mem, w_mem_kv, w_out, g_final):
    assert g_norm.shape[0] == 1
    b_p, l_p, _ = x_prompt.shape
    b_s_, l_s, _ = x_sample.shape
    weights = (g_ret[0], g_gmlp[0], w_s[0], b_s[0])

    mk, mv, mk16, mv16 = _memory_kv(mem_prompt, g_mem[0], w_mem_kv[0])

    head_major16 = lambda c: jnp.transpose(c[0], (0, 2, 1, 3)).astype(_BF16)
    w_in16, mix_s, s_s, gvn_s = _layer_sample(
        x_sample, head_major16(cache_mem_k), head_major16(cache_mem_v), state_ret[0], g_norm[0], w_in[0], weights,
        sp=SAMPLE_STEP)

    mix_p, s_p, w_out16 = _layer_prompt(x_prompt, mk16, mv16, g_norm[0], w_in16, w_out[0], weights,
                                        ts=PROMPT_TILE, lb=RET_BLOCK)
    y_p, y_s = _out_proj(x_prompt.reshape(b_p * l_p, D_MODEL), mix_p.reshape(b_p * l_p, MIX_WIDTH),
                         x_sample.reshape(b_s_ * l_s, D_MODEL), mix_s, w_out16, g_final, tile_rows=OUT_TILE)

    return (y_p.reshape(x_prompt.shape), y_s.reshape(x_sample.shape), s_p[None], mk[None], mv[None],
            s_s[None], gvn_s[None])
```

```python
import functools

import numpy as np

import jax
import jax.numpy as jnp
from jax import lax
from jax.experimental import pallas as pl
from jax.experimental.pallas import tpu as pltpu

D_MODEL = 2048
PAST_LEN = 1024
MEM_LEN = 256
RET_HEADS = 8
HEAD_DIM = 128
RET_WIDTH = RET_HEADS * HEAD_DIM
GMLP_GROUPS = 4
GMLP_WIDTH = GMLP_GROUPS * HEAD_DIM
GMLP_CHUNK = 128
XA_HEADS = 4
XA_WIDTH = XA_HEADS * HEAD_DIM
MIX_WIDTH = RET_WIDTH + GMLP_WIDTH + XA_WIDTH
ROPE_BASE = 10000.0
EPS = 1e-6
_EARLY_OPERAND_SCALE = 2.0 ** -32

_RQ, _RK, _RV, _RG = 0, RET_WIDTH, 2 * RET_WIDTH, 3 * RET_WIDTH
_GU = 4 * RET_WIDTH
_GV = _GU + GMLP_WIDTH
_GG = _GV + GMLP_WIDTH
_AQ = _GG + GMLP_WIDTH
_AG = _AQ + XA_WIDTH
IN_WIDTH = _AG + XA_WIDTH
W_SLAB = 4 * HEAD_DIM
WOUT_SLAB = HEAD_DIM
_MIX_GMLP = RET_WIDTH
_MIX_XA = RET_WIDTH + GMLP_WIDTH

_V7X_VMEM_LIMIT_BYTES = 60 * 1024 * 1024

_BF16 = jnp.bfloat16
_F32 = jnp.float32


def _dot(a, b):
    return jnp.dot(a, b, preferred_element_type=_F32)


def _dot_nt(a, b):
    return lax.dot_general(a, b, (((1,), (1,)), ((), ())), preferred_element_type=_F32)


def _dot_tn(a, b):
    return lax.dot_general(a, b, (((0,), (0,)), ((), ())), preferred_element_type=_F32)


def _silu(x):
    return x / (1.0 + jnp.exp(-x))


def _rms_scale(x):
    return x * lax.rsqrt(jnp.mean(x * x, axis=-1, keepdims=True) + EPS)


def _center_scale(x):
    mu = jnp.mean(x, axis=-1, keepdims=True)
    d = x - mu
    return d * lax.rsqrt(jnp.mean(d * d, axis=-1, keepdims=True) + EPS)


def _rope(u, cos2, sin2):
    return u * cos2 + pltpu.roll(u, HEAD_DIM // 2, 1) * sin2


def _cols(base, i):
    return slice(base + i * HEAD_DIM, base + (i + 1) * HEAD_DIM)


def _retention_blocks(blocks, dmask_ref, qdec_ref, kdec_ref, sdec_ref):
    sc = [_dot_nt(qb, kb.astype(_BF16)) for qb, kb, _, _, _ in blocks]
    cross = [_dot(qb, state.astype(_BF16)) for qb, _, _, state, _ in blocks]
    new_states = [state * sdec_ref[head] + _dot_tn((kb * kdec_ref[head]).astype(_BF16), vb)
                  for _, kb, vb, state, head in blocks]
    o = [_dot((s * dmask_ref[head]).astype(_BF16), vb) + c * qdec_ref[head]
         for s, c, (_, _, vb, _, head) in zip(sc, cross, blocks)]
    return o, new_states


def _fill_row_bias(dst_ref, bs_ref, n):
    row = lax.broadcasted_iota(jnp.int32, (n, n), 0)
    col = lax.broadcasted_iota(jnp.int32, (n, n), 1)
    for g in range(GMLP_GROUPS):
        diag = jnp.where(row == col, jnp.broadcast_to(bs_ref[g:g + 1, :n], (n, n)), 0.0)
        dst_ref[g] = jnp.broadcast_to(jnp.sum(diag, axis=1, keepdims=True), (n, HEAD_DIM))


def _tril_bf16(w):
    n = w.shape[0]
    row = lax.broadcasted_iota(jnp.int32, (n, n), 0)
    col = lax.broadcasted_iota(jnp.int32, (n, n), 1)
    return jnp.where(row >= col, w, 0.0).astype(_BF16)


def _xattn_heads(heads):
    sc = [_dot_nt(aq.astype(_BF16), mk) * (HEAD_DIM ** -0.5) for aq, mk, _ in heads]
    e = [jnp.exp(s - jnp.max(s, axis=-1, keepdims=True)) for s in sc]
    return [_dot(p.astype(_BF16), mv) / jnp.sum(p, axis=-1, keepdims=True) for p, (_, _, mv) in zip(e, heads)]


def _memkv_kernel(mem_ref, g_ref, w_ref, k_ref, v_ref, k16_ref, v16_ref, w16_s):
    @pl.when(pl.program_id(0) == 0)
    def _():
        w16_s[...] = w_ref[...].astype(_BF16)

    nb = mem_ref.shape[0]
    n = (_rms_scale(mem_ref[...].reshape(nb * MEM_LEN, D_MODEL)) * g_ref[...]).astype(_BF16)
    kv = _dot(n, w16_s[...])
    for i in range(nb):
        k = kv[i * MEM_LEN:(i + 1) * MEM_LEN, :XA_WIDTH]
        v = kv[i * MEM_LEN:(i + 1) * MEM_LEN, XA_WIDTH:]
        for hd in range(XA_HEADS):
            k_ref[i, :, hd, :] = k[:, _cols(0, hd)]
            v_ref[i, :, hd, :] = v[:, _cols(0, hd)]
        k16_ref[i] = k.astype(_BF16)
        v16_ref[i] = v.astype(_BF16)


def _prompt_kernel(x_ref, cos_ref, sin_ref, mk_ref, mv_ref, gnorm_ref, win_ref, gret_ref, ggm_ref, ws_ref, bs_ref,
                   dmask_ref, qdec_ref, kdec_ref, sdec_ref, wout_ref, mix_ref, s_ref, wout16_ref, h_s, bias_s,
                   *, lb, n_out_slabs):
    ts = x_ref.shape[1]
    step = pl.program_id(0) * pl.num_programs(1) + pl.program_id(1)

    @pl.when(step == 0)
    def _():
        _fill_row_bias(bias_s, bs_ref, GMLP_CHUNK)

    @pl.when(step < n_out_slabs)
    def _():
        wout16_ref[...] = wout_ref[...].astype(_BF16)

    @pl.when(pl.program_id(1) == 0)
    def _():
        s_ref[...] = jnp.zeros_like(s_ref)

    x = x_ref[0]
    h_s[...] = (x * (gnorm_ref[...] * _EARLY_OPERAND_SCALE)).astype(_BF16)
    row_scale = lax.rsqrt(jnp.mean(x * x, axis=-1, keepdims=True) + EPS)
    row_scale_b = jnp.broadcast_to(row_scale * (1.0 / _EARLY_OPERAND_SCALE), (ts, HEAD_DIM))

    def proj(c0, width, scaled_rows=True):
        raw = _dot(h_s[...], win_ref[:, c0:c0 + width])
        if scaled_rows:
            return raw
        return jnp.concatenate([raw[:, _cols(0, i)] * row_scale_b for i in range(width // HEAD_DIM)], axis=1)

    cos = cos_ref[...]
    sin = sin_ref[...]

    decay = (dmask_ref, qdec_ref, kdec_ref, sdec_ref)

    def retention_pair(p):
        c0 = 2 * HEAD_DIM * p
        q2, k2, v2, g2 = (proj(sec + c0, 2 * HEAD_DIM, scaled_rows=p > 0) for sec in (_RQ, _RK, _RV, _RG))
        if p == 0:
            h_s[...] = (x_ref[0] * row_scale * gnorm_ref[...]).astype(_BF16)
        heads = (2 * p, 2 * p + 1)
        qh = [_rope(q2[:, _cols(0, hh)], cos, sin).astype(_BF16) for hh in range(2)]
        kh = [_rope(k2[:, _cols(0, hh)], cos, sin) * (HEAD_DIM ** -0.5) for hh in range(2)]
        vh = [v2[:, _cols(0, hh)].astype(_BF16) for hh in range(2)]
        for bi in range(ts // lb):
            rs = slice(bi * lb, (bi + 1) * lb)
            o, new_states = _retention_blocks(
                [(qh[hh][rs], kh[hh][rs], vh[hh][rs], s_ref[0, head], head) for hh, head in enumerate(heads)],
                *decay)
            for hh, head in enumerate(heads):
                s_ref[0, head] = new_states[hh]
                on = _center_scale(o[hh]) * gret_ref[:, _cols(0, head)]
                mix_ref[0, rs, _cols(0, head)] = (on * _silu(g2[rs, _cols(0, hh)])).astype(_BF16)

    def gmlp():
        gvn = _center_scale(proj(_GV, GMLP_WIDTH)) * ggm_ref[...]
        gg = proj(_GG, GMLP_WIDTH)
        gu = proj(_GU, GMLP_WIDTH)
        for g in range(GMLP_GROUPS):
            hs = _cols(0, g)
            wm = _tril_bf16(ws_ref[g])
            for c in range(ts // GMLP_CHUNK):
                rs = slice(c * GMLP_CHUNK, (c + 1) * GMLP_CHUNK)
                sg = _dot(wm, gvn[rs, hs].astype(_BF16)) + bias_s[g]
                mix_ref[0, rs, _cols(_MIX_GMLP, g)] = (gu[rs, hs] * (sg * _silu(gg[rs, hs]))).astype(_BF16)

    def xattn():
        aq = proj(_AQ, XA_WIDTH)
        ag = proj(_AG, XA_WIDTH)
        ao = _xattn_heads([(aq[:, _cols(0, hd)], mk_ref[0, :, _cols(0, hd)], mv_ref[0, :, _cols(0, hd)])
                           for hd in range(XA_HEADS)])
        for hd in range(XA_HEADS):
            mix_ref[0, :, _cols(_MIX_XA, hd)] = (ao[hd] * _silu(ag[:, _cols(0, hd)])).astype(_BF16)

    for p in range(RET_HEADS // 2):
        retention_pair(p)
    xattn()
    gmlp()


class _SlabView:
    def __init__(self, ref):
        self.ref = ref

    def _at(self, rows, base, i):
        col = base + i * HEAD_DIM
        return (col // W_SLAB, rows, slice(col % W_SLAB, col % W_SLAB + HEAD_DIM))

    def get(self, rows, base, i):
        return self.ref[self._at(rows, base, i)]

    def set(self, rows, base, i, value):
        self.ref[self._at(rows, base, i)] = value


def _sample_kernel(x_ref, cos_ref, sin_ref, mk_ref, mv_ref, s0_ref, gnorm_ref, w_ref, gret_ref, ggm_ref, ws_ref,
                   bs_ref, dmask_ref, qdec_ref, kdec_ref, sdec_ref, w16_ref, mix_ref, s_ref, gvn_ref, h_s, proj_s,
                   bias_s, *, ts, sp, n_slabs):
    n = pl.program_id(0)
    proj = _SlabView(proj_s)
    all_rows = slice(None)

    @pl.when(n == 0)
    def _():
        h_s[...] = (_rms_scale(x_ref[...]) * gnorm_ref[...]).astype(_BF16)
        _fill_row_bias(bias_s, bs_ref, ts)

    @pl.when(n < n_slabs)
    def _():
        w16 = w_ref[...].astype(_BF16)
        w16_ref[...] = w16
        proj_s[n] = _dot(h_s[...], w16)

    @pl.when(n == n_slabs - 1)
    def _():
        cos = cos_ref[...]
        sin = sin_ref[...]
        for head in range(RET_HEADS):
            proj.set(all_rows, _RQ, head, _rope(proj.get(all_rows, _RQ, head), cos, sin))
            proj.set(all_rows, _RK, head, _rope(proj.get(all_rows, _RK, head), cos, sin) * (HEAD_DIM ** -0.5))
        gv = jnp.concatenate([proj.get(all_rows, _GV, g) for g in range(GMLP_GROUPS)], axis=1)
        gvn = _center_scale(gv) * ggm_ref[...]
        gvn_ref[...] = gvn
        for g in range(GMLP_GROUPS):
            proj.set(all_rows, _GV, g, gvn[:, _cols(0, g)])

    @pl.when(n >= n_slabs)
    def _():
        j = n - n_slabs
        rows = [pl.ds(pl.multiple_of((j * sp + i) * ts, ts), ts) for i in range(sp)]

        units = [(i, head) for i in range(sp) for head in range(RET_HEADS)]
        o, new_states = _retention_blocks(
            [(proj.get(rows[i], _RQ, head).astype(_BF16), proj.get(rows[i], _RK, head),
              proj.get(rows[i], _RV, head).astype(_BF16), s0_ref[i, head], head) for i, head in units],
            dmask_ref, qdec_ref, kdec_ref, sdec_ref)
        for (i, head), o_u, s_u in zip(units, o, new_states):
            s_ref[i, head] = s_u
            on = _center_scale(o_u) * gret_ref[:, _cols(0, head)]
            mix_ref[rows[i], _cols(0, head)] = (on * _silu(proj.get(rows[i], _RG, head))).astype(_BF16)

        for g in range(GMLP_GROUPS):
            wm = _tril_bf16(ws_ref[g, :ts, :ts])
            for i in range(sp):
                sg = _dot(wm, proj.get(rows[i], _GV, g).astype(_BF16)) + bias_s[g]
                mix_ref[rows[i], _cols(_MIX_GMLP, g)] = (
                    proj.get(rows[i], _GU, g) * sg * _silu(proj.get(rows[i], _GG, g))).astype(_BF16)

        units = [(i, hd) for i in range(sp) for hd in range(XA_HEADS)]
        ao = _xattn_heads([(proj.get(rows[i], _AQ, hd), mk_ref[i, hd], mv_ref[i, hd]) for i, hd in units])
        for (i, hd), ao_u in zip(units, ao):
            mix_ref[rows[i], _cols(_MIX_XA, hd)] = (ao_u * _silu(proj.get(rows[i], _AG, hd))).astype(_BF16)


def _outproj_kernel(xp_ref, mixp_ref, xs_ref, mixs_ref, w_ref, g_ref, yp_ref, ys_ref, *, n_prompt_tiles):
    n = pl.program_id(0)

    def project(x_ref, mix_ref, y_ref):
        y = x_ref[...] + _dot(mix_ref[...], w_ref[...])
        y_ref[...] = _rms_scale(y) * g_ref[...]

    @pl.when(n < n_prompt_tiles)
    def _():
        project(xp_ref, mixp_ref, yp_ref)

    @pl.when(n == n_prompt_tiles)
    def _():
        project(xs_ref, mixs_ref, ys_ref)


def _const_spec(shape):
    return pl.BlockSpec(shape, lambda *_: (0,) * len(shape), pipeline_mode=pl.Buffered(1))


def _rope_tables(pos):
    inv_freq = ROPE_BASE ** (-np.arange(0, HEAD_DIM, 2, dtype=np.float64) / HEAD_DIM)
    ang = np.asarray(pos, np.float64)[:, None] * inv_freq[None, :]
    cos, sin = np.cos(ang), np.sin(ang)
    return (np.concatenate([cos, cos], axis=-1).astype(np.float32),
            np.concatenate([-sin, sin], axis=-1).astype(np.float32))


def _decay_tables(lb):
    log_gamma = np.log(1.0 - 2.0 ** (-5.0 - np.arange(RET_HEADS, dtype=np.float64)))
    idx = np.arange(lb, dtype=np.float64)
    diff = idx[:, None] - idx[None, :]
    dmask = np.where(diff >= 0, np.exp(log_gamma[:, None, None] * np.maximum(diff, 0.0)[None]), 0.0)
    qdec = np.exp(log_gamma[:, None] * (idx[None, :] + 1.0))
    kdec = np.exp(log_gamma[:, None] * (lb - 1.0 - idx[None, :]))
    sdec = np.exp(log_gamma * lb)
    bcast = lambda a: np.broadcast_to(a[..., None], a.shape + (HEAD_DIM,)).astype(np.float32)
    return [dmask.astype(np.float32), bcast(qdec), bcast(kdec), bcast(sdec)[:, None, :]]


def _branch_consts(weights, lb):
    g_ret, g_gmlp, w_s, b_s = weights
    return [g_ret.reshape(1, -1), g_gmlp.reshape(1, -1), w_s, b_s] + _decay_tables(lb)


def _layer_prompt(x, mk16, mv16, g_norm, w_in16, w_out, weights, *, ts, lb):
    nb, l, _ = x.shape
    nt = l // ts
    n_out_slabs = D_MODEL // WOUT_SLAB
    assert n_out_slabs <= nb * nt
    tile = lambda width: pl.BlockSpec((1, ts, width), lambda b, t: (b, t, 0))
    per_stream = lambda shape: pl.BlockSpec(shape, lambda b, t: (b,) + (0,) * (len(shape) - 1))
    rope_spec = pl.BlockSpec((ts, HEAD_DIM), lambda b, t: (t, 0))
    out_slab_spec = pl.BlockSpec((MIX_WIDTH, WOUT_SLAB), lambda b, t: (0, jnp.minimum(b * nt + t, n_out_slabs - 1)))
    state_shape = (1, RET_HEADS, HEAD_DIM, HEAD_DIM)
    kv_spec = per_stream((1, MEM_LEN, XA_WIDTH))
    consts = [g_norm.reshape(1, -1), w_in16] + _branch_consts(weights, lb)
    return pl.pallas_call(
        functools.partial(_prompt_kernel, lb=lb, n_out_slabs=n_out_slabs),
        grid=(nb, nt),
        in_specs=[tile(D_MODEL), rope_spec, rope_spec, kv_spec, kv_spec] + [_const_spec(c.shape) for c in consts]
        + [out_slab_spec],
        out_specs=[tile(MIX_WIDTH), per_stream(state_shape), out_slab_spec],
        out_shape=[jax.ShapeDtypeStruct((nb, l, MIX_WIDTH), _BF16),
                   jax.ShapeDtypeStruct((nb,) + state_shape[1:], _F32),
                   jax.ShapeDtypeStruct(w_out.shape, _BF16)],
        scratch_shapes=[pltpu.VMEM((ts, D_MODEL), _BF16), pltpu.VMEM((GMLP_GROUPS, GMLP_CHUNK, HEAD_DIM), _F32)],
        compiler_params=pltpu.CompilerParams(
            dimension_semantics=("arbitrary", "arbitrary"), vmem_limit_bytes=_V7X_VMEM_LIMIT_BYTES),
        name="layer_prompt",
    )(x, *_rope_tables(np.arange(l)), mk16, mv16, *consts, w_out)


def _layer_sample(x, cache_k, cache_v, s0, g_norm, w_in, weights, *, sp):
    n, ts, _ = x.shape
    m = n * ts
    n_slabs = IN_WIDTH // W_SLAB
    branch_step = lambda i: jnp.maximum(i - n_slabs, 0)
    per_step = lambda shape: pl.BlockSpec(shape, lambda i: (branch_step(i),) + (0,) * (len(shape) - 1))
    slab_spec = pl.BlockSpec((D_MODEL, W_SLAB), lambda i: (0, jnp.minimum(i, n_slabs - 1)))
    state_shape = (sp, RET_HEADS, HEAD_DIM, HEAD_DIM)
    kv_spec = per_step((sp, XA_HEADS, MEM_LEN, HEAD_DIM))
    consts = _branch_consts(weights, ts)
    w_in16, mix, s_new, gvn = pl.pallas_call(
        functools.partial(_sample_kernel, ts=ts, sp=sp, n_slabs=n_slabs),
        grid=(n_slabs + n // sp,),
        in_specs=[_const_spec((m, D_MODEL)), _const_spec((m, HEAD_DIM)), _const_spec((m, HEAD_DIM)), kv_spec, kv_spec,
                  per_step(state_shape), _const_spec((1, D_MODEL)), slab_spec]
        + [_const_spec(c.shape) for c in consts],
        out_specs=[slab_spec, pl.BlockSpec((m, MIX_WIDTH), lambda i: (0, 0)), per_step(state_shape),
                   pl.BlockSpec((m, GMLP_WIDTH), lambda i: (0, 0))],
        out_shape=[jax.ShapeDtypeStruct(w_in.shape, _BF16),
                   jax.ShapeDtypeStruct((m, MIX_WIDTH), _BF16),
                   jax.ShapeDtypeStruct((n,) + state_shape[1:], _F32),
                   jax.ShapeDtypeStruct((m, GMLP_WIDTH), _F32)],
        scratch_shapes=[pltpu.VMEM((m, D_MODEL), _BF16), pltpu.VMEM((n_slabs, m, W_SLAB), _F32),
                        pltpu.VMEM((GMLP_GROUPS, ts, HEAD_DIM), _F32)],
        compiler_params=pltpu.CompilerParams(
            dimension_semantics=("arbitrary",), vmem_limit_bytes=_V7X_VMEM_LIMIT_BYTES),
        name="layer_sample",
    )(x.reshape(m, D_MODEL), *_rope_tables(np.tile(PAST_LEN + np.arange(ts), n)), cache_k, cache_v, s0,
      g_norm.reshape(1, -1), w_in, *consts)
    return w_in16, mix, s_new, gvn.reshape(n, ts, GMLP_WIDTH)


def _out_proj(x_p, mix_p, x_s, mix_s, w_out16, g_final, *, tile_rows):
    n_p = x_p.shape[0] // tile_rows
    p_blk = pl.BlockSpec((tile_rows, D_MODEL), lambda n: (jnp.minimum(n, n_p - 1), 0))
    s_out_blk = pl.BlockSpec(x_s.shape, lambda n: (0, 0))
    return pl.pallas_call(
        functools.partial(_outproj_kernel, n_prompt_tiles=n_p),
        grid=(n_p + 1,),
        in_specs=[p_blk, p_blk, _const_spec(x_s.shape), _const_spec(mix_s.shape), _const_spec(w_out16.shape),
                  _const_spec((1, D_MODEL))],
        out_specs=[p_blk, s_out_blk],
        out_shape=[jax.ShapeDtypeStruct(x_p.shape, _F32), jax.ShapeDtypeStruct(x_s.shape, _F32)],
        compiler_params=pltpu.CompilerParams(
            dimension_semantics=("arbitrary",), vmem_limit_bytes=_V7X_VMEM_LIMIT_BYTES),
        name="out_proj",
    )(x_p, mix_p, x_s, mix_s, w_out16, g_final.reshape(1, -1))


def _memory_kv(mem, g_mem, w_mem_kv):
    b = mem.shape[0]
    nb = MEMKV_STREAMS
    blk = lambda width: pl.BlockSpec((nb, MEM_LEN, width), lambda i: (i, 0, 0))
    blk4 = pl.BlockSpec((nb, MEM_LEN, XA_HEADS, HEAD_DIM), lambda i: (i, 0, 0, 0))
    kv_f32 = jax.ShapeDtypeStruct((b, MEM_LEN, XA_HEADS, HEAD_DIM), _F32)
    kv_b16 = jax.ShapeDtypeStruct((b, MEM_LEN, XA_WIDTH), _BF16)
    return pl.pallas_call(
        _memkv_kernel,
        grid=(b // nb,),
        in_specs=[blk(D_MODEL), _const_spec((1, D_MODEL)), _const_spec(w_mem_kv.shape)],
        out_specs=[blk4, blk4, blk(XA_WIDTH), blk(XA_WIDTH)],
        out_shape=[kv_f32, kv_f32, kv_b16, kv_b16],
        scratch_shapes=[pltpu.VMEM(w_mem_kv.shape, _BF16)],
        compiler_params=pltpu.CompilerParams(dimension_semantics=("arbitrary",)),
        name="memory_kv",
    )(mem, g_mem.reshape(1, -1), w_mem_kv)


MEMKV_STREAMS = 2
PROMPT_TILE = 512
RET_BLOCK = 256
OUT_TILE = 512
SAMPLE_STEP = 4


def kernel(x_prompt, x_sample, mem_prompt, state_ret, cache_mem_k, cache_mem_v, g_norm, w_in, g_ret, g_gmlp,
           w_s, b_s, g_mem, w_mem_kv, w_out, g_final):
    assert g_norm.shape[0] == 1
    b_p, l_p, _ = x_prompt.shape
    b_s_, l_s, _ = x_sample.shape
    weights = (g_ret[0], g_gmlp[0], w_s[0], b_s[0])

    mk, mv, mk16, mv16 = _memory_kv(mem_prompt, g_mem[0], w_mem_kv[0])

    head_major16 = lambda c: jnp.transpose(c[0], (0, 2, 1, 3)).astype(_BF16)
    w_in16, mix_s, s_s, gvn_s = _layer_sample(
        x_sample, head_major16(cache_mem_k), head_major16(cache_mem_v), state_ret[0], g_norm[0], w_in[0], weights,
        sp=SAMPLE_STEP)

    mix_p, s_p, w_out16 = _layer_prompt(x_prompt, mk16, mv16, g_norm[0], w_in16, w_out[0], weights,
                                        ts=PROMPT_TILE, lb=RET_BLOCK)
    y_p, y_s = _out_proj(x_prompt.reshape(b_p * l_p, D_MODEL), mix_p.reshape(b_p * l_p, MIX_WIDTH),
                         x_sample.reshape(b_s_ * l_s, D_MODEL), mix_s, w_out16, g_final, tile_rows=OUT_TILE)

    return (y_p.reshape(x_prompt.shape), y_s.reshape(x_sample.shape), s_p[None], mk[None], mv[None],
            s_s[None], gvn_s[None])
```

```python
import functools

import numpy as np

import jax
import jax.numpy as jnp
from jax import lax
from jax.experimental import pallas as pl
from jax.experimental.pallas import tpu as pltpu

D_MODEL = 2048
PAST_LEN = 1024
MEM_LEN = 256
RET_HEADS = 8
HEAD_DIM = 128
RET_WIDTH = RET_HEADS * HEAD_DIM
GMLP_GROUPS = 4
GMLP_WIDTH = GMLP_GROUPS * HEAD_DIM
GMLP_CHUNK = 128
XA_HEADS = 4
XA_WIDTH = XA_HEADS * HEAD_DIM
MIX_WIDTH = RET_WIDTH + GMLP_WIDTH + XA_WIDTH
ROPE_BASE = 10000.0
EPS = 1e-6
_EARLY_OPERAND_SCALE = 2.0 ** -32

_RQ, _RK, _RV, _RG = 0, RET_WIDTH, 2 * RET_WIDTH, 3 * RET_WIDTH
_GU = 4 * RET_WIDTH
_GV = _GU + GMLP_WIDTH
_GG = _GV + GMLP_WIDTH
_AQ = _GG + GMLP_WIDTH
_AG = _AQ + XA_WIDTH
IN_WIDTH = _AG + XA_WIDTH
W_SLAB = 4 * HEAD_DIM
WOUT_SLAB = HEAD_DIM
_MIX_GMLP = RET_WIDTH
_MIX_XA = RET_WIDTH + GMLP_WIDTH

_V7X_VMEM_LIMIT_BYTES = 60 * 1024 * 1024

_BF16 = jnp.bfloat16
_F32 = jnp.float32


def _dot(a, b):
    return jnp.dot(a, b, preferred_element_type=_F32)


def _dot_nt(a, b):
    return lax.dot_general(a, b, (((1,), (1,)), ((), ())), preferred_element_type=_F32)


def _dot_tn(a, b):
    return lax.dot_general(a, b, (((0,), (0,)), ((), ())), preferred_element_type=_F32)


def _silu(x):
    return x / (1.0 + jnp.exp(-x))


def _rms_scale(x):
    return x * lax.rsqrt(jnp.mean(x * x, axis=-1, keepdims=True) + EPS)


def _center_scale(x):
    mu = jnp.mean(x, axis=-1, keepdims=True)
    d = x - mu
    return d * lax.rsqrt(jnp.mean(d * d, axis=-1, keepdims=True) + EPS)


def _rope(u, cos2, sin2):
    return u * cos2 + pltpu.roll(u, HEAD_DIM // 2, 1) * sin2


def _cols(base, i):
    return slice(base + i * HEAD_DIM, base + (i + 1) * HEAD_DIM)


def _retention_blocks(blocks, dmask_ref, qdec_ref, kdec_ref, sdec_ref):
    sc = [_dot_nt(qb, kb.astype(_BF16)) for qb, kb, _, _, _ in blocks]
    cross = [_dot(qb, state.astype(_BF16)) for qb, _, _, state, _ in blocks]
    new_states = [state * sdec_ref[head] + _dot_tn((kb * kdec_ref[head]).astype(_BF16), vb)
                  for _, kb, vb, state, head in blocks]
    o = [_dot((s * dmask_ref[head]).astype(_BF16), vb) + c * qdec_ref[head]
         for s, c, (_, _, vb, _, head) in zip(sc, cross, blocks)]
    return o, new_states


def _fill_row_bias(dst_ref, bs_ref, n):
    row = lax.broadcasted_iota(jnp.int32, (n, n), 0)
    col = lax.broadcasted_iota(jnp.int32, (n, n), 1)
    for g in range(GMLP_GROUPS):
        diag = jnp.where(row == col, jnp.broadcast_to(bs_ref[g:g + 1, :n], (n, n)), 0.0)
        dst_ref[g] = jnp.broadcast_to(jnp.sum(diag, axis=1, keepdims=True), (n, HEAD_DIM))


def _tril_bf16(w):
    n = w.shape[0]
    row = lax.broadcasted_iota(jnp.int32, (n, n), 0)
    col = lax.broadcasted_iota(jnp.int32, (n, n), 1)
    return jnp.where(row >= col, w, 0.0).astype(_BF16)


def _xattn_heads(heads):
    sc = [_dot_nt(aq.astype(_BF16), mk) * (HEAD_DIM ** -0.5) for aq, mk, _ in heads]
    e = [jnp.exp(s - jnp.max(s, axis=-1, keepdims=True)) for s in sc]
    return [_dot(p.astype(_BF16), mv) / jnp.sum(p, axis=-1, keepdims=True) for p, (_, _, mv) in zip(e, heads)]


def _memkv_kernel(mem_ref, g_ref, w_ref, k_ref, v_ref, k16_ref, v16_ref, w16_s):
    @pl.when(pl.program_id(0) == 0)
    def _():
        w16_s[...] = w_ref[...].astype(_BF16)

    nb = mem_ref.shape[0]
    n = (_rms_scale(mem_ref[...].reshape(nb * MEM_LEN, D_MODEL)) * g_ref[...]).astype(_BF16)
    kv = _dot(n, w16_s[...])
    for i in range(nb):
        k = kv[i * MEM_LEN:(i + 1) * MEM_LEN, :XA_WIDTH]
        v = kv[i * MEM_LEN:(i + 1) * MEM_LEN, XA_WIDTH:]
        for hd in range(XA_HEADS):
            k_ref[i, :, hd, :] = k[:, _cols(0, hd)]
            v_ref[i, :, hd, :] = v[:, _cols(0, hd)]
        k16_ref[i] = k.astype(_BF16)
        v16_ref[i] = v.astype(_BF16)


def _prompt_kernel(x_ref, cos_ref, sin_ref, mk_ref, mv_ref, gnorm_ref, win_ref, gret_ref, ggm_ref, ws_ref, bs_ref,
                   dmask_ref, qdec_ref, kdec_ref, sdec_ref, wout_ref, mix_ref, s_ref, wout16_ref, h_s, bias_s,
                   *, lb, n_out_slabs):
    ts = x_ref.shape[1]
    step = pl.program_id(0) * pl.num_programs(1) + pl.program_id(1)

    @pl.when(step == 0)
    def _():
        _fill_row_bias(bias_s, bs_ref, GMLP_CHUNK)

    @pl.when(step < n_out_slabs)
    def _():
        wout16_ref[...] = wout_ref[...].astype(_BF16)

    @pl.when(pl.program_id(1) == 0)
    def _():
        s_ref[...] = jnp.zeros_like(s_ref)

    x = x_ref[0]
    h_s[...] = (x * (gnorm_ref[...] * _EARLY_OPERAND_SCALE)).astype(_BF16)
    row_scale = lax.rsqrt(jnp.mean(x * x, axis=-1, keepdims=True) + EPS)
    row_scale_b = jnp.broadcast_to(row_scale * (1.0 / _EARLY_OPERAND_SCALE), (ts, HEAD_DIM))

    def proj(c0, width, scaled_rows=True):
        raw = _dot(h_s[...], win_ref[:, c0:c0 + width])
        if scaled_rows:
            return raw
        return jnp.concatenate([raw[:, _cols(0, i)] * row_scale_b for i in range(width // HEAD_DIM)], axis=1)

    cos = cos_ref[...]
    sin = sin_ref[...]

    decay = (dmask_ref, qdec_ref, kdec_ref, sdec_ref)

    def retention_pair(p):
        c0 = 2 * HEAD_DIM * p
        q2, k2, v2, g2 = (proj(sec + c0, 2 * HEAD_DIM, scaled_rows=p > 0) for sec in (_RQ, _RK, _RV, _RG))
        if p == 0:
            h_s[...] = (x_ref[0] * row_scale * gnorm_ref[...]).astype(_BF16)
        heads = (2 * p, 2 * p + 1)
        qh = [_rope(q2[:, _cols(0, hh)], cos, sin).astype(_BF16) for hh in range(2)]
        kh = [_rope(k2[:, _cols(0, hh)], cos, sin) * (HEAD_DIM ** -0.5) for hh in range(2)]
        vh = [v2[:, _cols(0, hh)].astype(_BF16) for hh in range(2)]
        for bi in range(ts // lb):
            rs = slice(bi * lb, (bi + 1) * lb)
            o, new_states = _retention_blocks(
                [(qh[hh][rs], kh[hh][rs], vh[hh][rs], s_ref[0, head], head) for hh, head in enumerate(heads)],
                *decay)
            for hh, head in enumerate(heads):
                s_ref[0, head] = new_states[hh]
                on = _center_scale(o[hh]) * gret_ref[:, _cols(0, head)]
                mix_ref[0, rs, _cols(0, head)] = (on * _silu(g2[rs, _cols(0, hh)])).astype(_BF16)

    def gmlp():
        gvn = _center_scale(proj(_GV, GMLP_WIDTH)) * ggm_ref[...]
        gg = proj(_GG, GMLP_WIDTH)
        gu = proj(_GU, GMLP_WIDTH)
        for g in range(GMLP_GROUPS):
            hs = _cols(0, g)
            wm = _tril_bf16(ws_ref[g])
            for c in range(ts // GMLP_CHUNK):
                rs = slice(c * GMLP_CHUNK, (c + 1) * GMLP_CHUNK)
                sg = _dot(wm, gvn[rs, hs].astype(_BF16)) + bias_s[g]
                mix_ref[0, rs, _cols(_MIX_GMLP, g)] = (gu[rs, hs] * (sg * _silu(gg[rs, hs]))).astype(_BF16)

    def xattn():
        aq = proj(_AQ, XA_WIDTH)
        ag = proj(_AG, XA_WIDTH)
        ao = _xattn_heads([(aq[:, _cols(0, hd)], mk_ref[0, :, _cols(0, hd)], mv_ref[0, :, _cols(0, hd)])
                           for hd in range(XA_HEADS)])
        for hd in range(XA_HEADS):
            mix_ref[0, :, _cols(_MIX_XA, hd)] = (ao[hd] * _silu(ag[:, _cols(0, hd)])).astype(_BF16)

    for p in range(RET_HEADS // 2):
        retention_pair(p)
    xattn()
    gmlp()


class _SlabView:
    def __init__(self, ref):
        self.ref = ref

    def _at(self, rows, base, i):
        col = base + i * HEAD_DIM
        return (col // W_SLAB, rows, slice(col % W_SLAB, col % W_SLAB + HEAD_DIM))

    def get(self, rows, base, i):
        return self.ref[self._at(rows, base, i)]

    def set(self, rows, base, i, value):
        self.ref[self._at(rows, base, i)] = value


def _sample_kernel(x_ref, cos_ref, sin_ref, ck_hbm, cv_hbm, s0_ref, gnorm_ref, w_ref, gret_ref, ggm_ref, ws_ref,
                   bs_ref, dmask_ref, qdec_ref, kdec_ref, sdec_ref, w16_ref, mix_ref, s_ref, gvn_ref, h_s, proj_s,
                   bias_s, kbuf, vbuf, kv_sem, *, ts, sp, n_slabs, n_branch):
    n = pl.program_id(0)
    proj = _SlabView(proj_s)
    all_rows = slice(None)

    def kv_copies(j, slot):
        copies = []
        for i in range(sp):
            for hd in range(XA_HEADS):
                for a, (src, dst) in enumerate(((ck_hbm, kbuf), (cv_hbm, vbuf))):
                    copies.append(pltpu.make_async_copy(
                        src.at[j * sp + i, :, hd, :], dst.at[slot, i * XA_HEADS + hd], kv_sem.at[slot, a]))
        return copies

    @pl.when(n == 0)
    def _():
        for copy in kv_copies(0, 0):
            copy.start()
        h_s[...] = (_rms_scale(x_ref[...]) * gnorm_ref[...]).astype(_BF16)
        _fill_row_bias(bias_s, bs_ref, ts)

    @pl.when(n < n_slabs)
    def _():
        w16 = w_ref[...].astype(_BF16)
        w16_ref[...] = w16
        proj_s[n] = _dot(h_s[...], w16)

    @pl.when(n == n_slabs - 1)
    def _():
        cos = cos_ref[...]
        sin = sin_ref[...]
        for head in range(RET_HEADS):
            proj.set(all_rows, _RQ, head, _rope(proj.get(all_rows, _RQ, head), cos, sin))
            proj.set(all_rows, _RK, head, _rope(proj.get(all_rows, _RK, head), cos, sin) * (HEAD_DIM ** -0.5))
        gv = jnp.concatenate([proj.get(all_rows, _GV, g) for g in range(GMLP_GROUPS)], axis=1)
        gvn = _center_scale(gv) * ggm_ref[...]
        gvn_ref[...] = gvn
        for g in range(GMLP_GROUPS):
            proj.set(all_rows, _GV, g, gvn[:, _cols(0, g)])

    @pl.when(n >= n_slabs)
    def _():
        j = n - n_slabs
        slot = j % 2
        rows = [pl.ds(pl.multiple_of((j * sp + i) * ts, ts), ts) for i in range(sp)]

        @pl.when(j + 1 < n_branch)
        def _():
            for copy in kv_copies(j + 1, 1 - slot):
                copy.start()

        for copy in kv_copies(j, slot):
            copy.wait()

        units = [(i, head) for i in range(sp) for head in range(RET_HEADS)]
        o, new_states = _retention_blocks(
            [(proj.get(rows[i], _RQ, head).astype(_BF16), proj.get(rows[i], _RK, head),
              proj.get(rows[i], _RV, head).astype(_BF16), s0_ref[i, head], head) for i, head in units],
            dmask_ref, qdec_ref, kdec_ref, sdec_ref)
        for (i, head), o_u, s_u in zip(units, o, new_states):
            s_ref[i, head] = s_u
            on = _center_scale(o_u) * gret_ref[:, _cols(0, head)]
            mix_ref[rows[i], _cols(0, head)] = (on * _silu(proj.get(rows[i], _RG, head))).astype(_BF16)

        for g in range(GMLP_GROUPS):
            wm = _tril_bf16(ws_ref[g, :ts, :ts])
            for i in range(sp):
                sg = _dot(wm, proj.get(rows[i], _GV, g).astype(_BF16)) + bias_s[g]
                mix_ref[rows[i], _cols(_MIX_GMLP, g)] = (
                    proj.get(rows[i], _GU, g) * sg * _silu(proj.get(rows[i], _GG, g))).astype(_BF16)

        units = [(i, hd) for i in range(sp) for hd in range(XA_HEADS)]
        ao = _xattn_heads([(proj.get(rows[i], _AQ, hd), kbuf[slot, i * XA_HEADS + hd].astype(_BF16),
                            vbuf[slot, i * XA_HEADS + hd].astype(_BF16)) for i, hd in units])
        for (i, hd), ao_u in zip(units, ao):
            mix_ref[rows[i], _cols(_MIX_XA, hd)] = (ao_u * _silu(proj.get(rows[i], _AG, hd))).astype(_BF16)


def _outproj_kernel(xp_ref, mixp_ref, xs_ref, mixs_ref, w_ref, g_ref, yp_ref, ys_ref, *, n_prompt_tiles):
    n = pl.program_id(0)

    def project(x_ref, mix_ref, y_ref):
        y = x_ref[...] + _dot(mix_ref[...], w_ref[...])
        y_ref[...] = _rms_scale(y) * g_ref[...]

    @pl.when(n < n_prompt_tiles)
    def _():
        project(xp_ref, mixp_ref, yp_ref)

    @pl.when(n == n_prompt_tiles)
    def _():
        project(xs_ref, mixs_ref, ys_ref)


def _const_spec(shape):
    return pl.BlockSpec(shape, lambda *_: (0,) * len(shape), pipeline_mode=pl.Buffered(1))


def _rope_tables(pos):
    inv_freq = ROPE_BASE ** (-np.arange(0, HEAD_DIM, 2, dtype=np.float64) / HEAD_DIM)
    ang = np.asarray(pos, np.float64)[:, None] * inv_freq[None, :]
    cos, sin = np.cos(ang), np.sin(ang)
    return (np.concatenate([cos, cos], axis=-1).astype(np.float32),
            np.concatenate([-sin, sin], axis=-1).astype(np.float32))


def _decay_tables(lb):
    log_gamma = np.log(1.0 - 2.0 ** (-5.0 - np.arange(RET_HEADS, dtype=np.float64)))
    idx = np.arange(lb, dtype=np.float64)
    diff = idx[:, None] - idx[None, :]
    dmask = np.where(diff >= 0, np.exp(log_gamma[:, None, None] * np.maximum(diff, 0.0)[None]), 0.0)
    qdec = np.exp(log_gamma[:, None] * (idx[None, :] + 1.0))
    kdec = np.exp(log_gamma[:, None] * (lb - 1.0 - idx[None, :]))
    sdec = np.exp(log_gamma * lb)
    bcast = lambda a: np.broadcast_to(a[..., None], a.shape + (HEAD_DIM,)).astype(np.float32)
    return [dmask.astype(np.float32), bcast(qdec), bcast(kdec), bcast(sdec)[:, None, :]]


def _branch_consts(weights, lb):
    g_ret, g_gmlp, w_s, b_s = weights
    return [g_ret.reshape(1, -1), g_gmlp.reshape(1, -1), w_s, b_s] + _decay_tables(lb)


def _layer_prompt(x, mk16, mv16, g_norm, w_in16, w_out, weights, *, ts, lb):
    nb, l, _ = x.shape
    nt = l // ts
    n_out_slabs = D_MODEL // WOUT_SLAB
    assert n_out_slabs <= nb * nt
    tile = lambda width: pl.BlockSpec((1, ts, width), lambda b, t: (b, t, 0))
    per_stream = lambda shape: pl.BlockSpec(shape, lambda b, t: (b,) + (0,) * (len(shape) - 1))
    rope_spec = pl.BlockSpec((ts, HEAD_DIM), lambda b, t: (t, 0))
    out_slab_spec = pl.BlockSpec((MIX_WIDTH, WOUT_SLAB), lambda b, t: (0, jnp.minimum(b * nt + t, n_out_slabs - 1)))
    state_shape = (1, RET_HEADS, HEAD_DIM, HEAD_DIM)
    kv_spec = per_stream((1, MEM_LEN, XA_WIDTH))
    consts = [g_norm.reshape(1, -1), w_in16] + _branch_consts(weights, lb)
    return pl.pallas_call(
        functools.partial(_prompt_kernel, lb=lb, n_out_slabs=n_out_slabs),
        grid=(nb, nt),
        in_specs=[tile(D_MODEL), rope_spec, rope_spec, kv_spec, kv_spec] + [_const_spec(c.shape) for c in consts]
        + [out_slab_spec],
        out_specs=[tile(MIX_WIDTH), per_stream(state_shape), out_slab_spec],
        out_shape=[jax.ShapeDtypeStruct((nb, l, MIX_WIDTH), _BF16),
                   jax.ShapeDtypeStruct((nb,) + state_shape[1:], _F32),
                   jax.ShapeDtypeStruct(w_out.shape, _BF16)],
        scratch_shapes=[pltpu.VMEM((ts, D_MODEL), _BF16), pltpu.VMEM((GMLP_GROUPS, GMLP_CHUNK, HEAD_DIM), _F32)],
        compiler_params=pltpu.CompilerParams(
            dimension_semantics=("arbitrary", "arbitrary"), vmem_limit_bytes=_V7X_VMEM_LIMIT_BYTES),
        name="layer_prompt",
    )(x, *_rope_tables(np.arange(l)), mk16, mv16, *consts, w_out)


def _layer_sample(x, cache_k, cache_v, s0, g_norm, w_in, weights, *, sp):
    n, ts, _ = x.shape
    m = n * ts
    n_slabs = IN_WIDTH // W_SLAB
    branch_step = lambda i: jnp.maximum(i - n_slabs, 0)
    per_step = lambda shape: pl.BlockSpec(shape, lambda i: (branch_step(i),) + (0,) * (len(shape) - 1))
    slab_spec = pl.BlockSpec((D_MODEL, W_SLAB), lambda i: (0, jnp.minimum(i, n_slabs - 1)))
    state_shape = (sp, RET_HEADS, HEAD_DIM, HEAD_DIM)
    kv_spec = pl.BlockSpec(memory_space=pl.ANY)
    kv_buf = pltpu.VMEM((2, sp * XA_HEADS, MEM_LEN, HEAD_DIM), _F32)
    consts = _branch_consts(weights, ts)
    w_in16, mix, s_new, gvn = pl.pallas_call(
        functools.partial(_sample_kernel, ts=ts, sp=sp, n_slabs=n_slabs, n_branch=n // sp),
        grid=(n_slabs + n // sp,),
        in_specs=[_const_spec((m, D_MODEL)), _const_spec((m, HEAD_DIM)), _const_spec((m, HEAD_DIM)), kv_spec, kv_spec,
                  per_step(state_shape), _const_spec((1, D_MODEL)), slab_spec]
        + [_const_spec(c.shape) for c in consts],
        out_specs=[slab_spec, pl.BlockSpec((m, MIX_WIDTH), lambda i: (0, 0)), per_step(state_shape),
                   pl.BlockSpec((m, GMLP_WIDTH), lambda i: (0, 0))],
        out_shape=[jax.ShapeDtypeStruct(w_in.shape, _BF16),
                   jax.ShapeDtypeStruct((m, MIX_WIDTH), _BF16),
                   jax.ShapeDtypeStruct((n,) + state_shape[1:], _F32),
                   jax.ShapeDtypeStruct((m, GMLP_WIDTH), _F32)],
        scratch_shapes=[pltpu.VMEM((m, D_MODEL), _BF16), pltpu.VMEM((n_slabs, m, W_SLAB), _F32),
                        pltpu.VMEM((GMLP_GROUPS, ts, HEAD_DIM), _F32), kv_buf, kv_buf,
                        pltpu.SemaphoreType.DMA((2, 2))],
        compiler_params=pltpu.CompilerParams(
            dimension_semantics=("arbitrary",), vmem_limit_bytes=_V7X_VMEM_LIMIT_BYTES),
        name="layer_sample",
    )(x.reshape(m, D_MODEL), *_rope_tables(np.tile(PAST_LEN + np.arange(ts), n)), cache_k, cache_v, s0,
      g_norm.reshape(1, -1), w_in, *consts)
    return w_in16, mix, s_new, gvn.reshape(n, ts, GMLP_WIDTH)


def _out_proj(x_p, mix_p, x_s, mix_s, w_out16, g_final, *, tile_rows):
    n_p = x_p.shape[0] // tile_rows
    p_blk = pl.BlockSpec((tile_rows, D_MODEL), lambda n: (jnp.minimum(n, n_p - 1), 0))
    s_out_blk = pl.BlockSpec(x_s.shape, lambda n: (0, 0))
    return pl.pallas_call(
        functools.partial(_outproj_kernel, n_prompt_tiles=n_p),
        grid=(n_p + 1,),
        in_specs=[p_blk, p_blk, _const_spec(x_s.shape), _const_spec(mix_s.shape), _const_spec(w_out16.shape),
                  _const_spec((1, D_MODEL))],
        out_specs=[p_blk, s_out_blk],
        out_shape=[jax.ShapeDtypeStruct(x_p.shape, _F32), jax.ShapeDtypeStruct(x_s.shape, _F32)],
        compiler_params=pltpu.CompilerParams(
            dimension_semantics=("arbitrary",), vmem_limit_bytes=_V7X_VMEM_LIMIT_BYTES),
        name="out_proj",
    )(x_p, mix_p, x_s, mix_s, w_out16, g_final.reshape(1, -1))


def _memory_kv(mem, g_mem, w_mem_kv):
    b = mem.shape[0]
    nb = MEMKV_STREAMS
    blk = lambda width: pl.BlockSpec((nb, MEM_LEN, width), lambda i: (i, 0, 0))
    blk4 = pl.BlockSpec((nb, MEM_LEN, XA_HEADS, HEAD_DIM), lambda i: (i, 0, 0, 0))
    kv_f32 = jax.ShapeDtypeStruct((b, MEM_LEN, XA_HEADS, HEAD_DIM), _F32)
    kv_b16 = jax.ShapeDtypeStruct((b, MEM_LEN, XA_WIDTH), _BF16)
    return pl.pallas_call(
        _memkv_kernel,
        grid=(b // nb,),
        in_specs=[blk(D_MODEL), _const_spec((1, D_MODEL)), _const_spec(w_mem_kv.shape)],
        out_specs=[blk4, blk4, blk(XA_WIDTH), blk(XA_WIDTH)],
        out_shape=[kv_f32, kv_f32, kv_b16, kv_b16],
        scratch_shapes=[pltpu.VMEM(w_mem_kv.shape, _BF16)],
        compiler_params=pltpu.CompilerParams(dimension_semantics=("arbitrary",)),
        name="memory_kv",
    )(mem, g_mem.reshape(1, -1), w_mem_kv)


MEMKV_STREAMS = 2
PROMPT_TILE = 512
RET_BLOCK = 256
OUT_TILE = 512
SAMPLE_STEP = 4


def kernel(x_prompt, x_sample, mem_prompt, state_ret, cache_mem_k, cache_mem_v, g_norm, w_in, g_ret, g_gmlp,
           w_s, b_s, g_mem, w_mem_kv, w_out, g_final):
    assert g_norm.shape[0] == 1
    b_p, l_p, _ = x_prompt.shape
    b_s_, l_s, _ = x_sample.shape
    weights = (g_ret[0], g_gmlp[0], w_s[0], b_s[0])

    mk, mv, mk16, mv16 = _memory_kv(mem_prompt, g_mem[0], w_mem_kv[0])

    w_in16, mix_s, s_s, gvn_s = _layer_sample(
        x_sample, cache_mem_k[0], cache_mem_v[0], state_ret[0], g_norm[0], w_in[0], weights, sp=SAMPLE_STEP)

    mix_p, s_p, w_out16 = _layer_prompt(x_prompt, mk16, mv16, g_norm[0], w_in16, w_out[0], weights,
                                        ts=PROMPT_TILE, lb=RET_BLOCK)
    y_p, y_s = _out_proj(x_prompt.reshape(b_p * l_p, D_MODEL), mix_p.reshape(b_p * l_p, MIX_WIDTH),
                         x_sample.reshape(b_s_ * l_s, D_MODEL), mix_s, w_out16, g_final, tile_rows=OUT_TILE)

    return (y_p.reshape(x_prompt.shape), y_s.reshape(x_sample.shape), s_p[None], mk[None], mv[None],
            s_s[None], gvn_s[None])
```

```python
import functools

import numpy as np

import jax
import jax.numpy as jnp
from jax import lax
from jax.experimental import pallas as pl
from jax.experimental.pallas import tpu as pltpu

D_MODEL = 2048
PAST_LEN = 1024
MEM_LEN = 256
RET_HEADS = 8
HEAD_DIM = 128
RET_WIDTH = RET_HEADS * HEAD_DIM
GMLP_GROUPS = 4
GMLP_WIDTH = GMLP_GROUPS * HEAD_DIM
GMLP_CHUNK = 128
XA_HEADS = 4
XA_WIDTH = XA_HEADS * HEAD_DIM
MIX_WIDTH = RET_WIDTH + GMLP_WIDTH + XA_WIDTH
ROPE_BASE = 10000.0
EPS = 1e-6
_EARLY_OPERAND_SCALE = 2.0 ** -32

_RQ, _RK, _RV, _RG = 0, RET_WIDTH, 2 * RET_WIDTH, 3 * RET_WIDTH
_GU = 4 * RET_WIDTH
_GV = _GU + GMLP_WIDTH
_GG = _GV + GMLP_WIDTH
_AQ = _GG + GMLP_WIDTH
_AG = _AQ + XA_WIDTH
IN_WIDTH = _AG + XA_WIDTH
W_SLAB = 4 * HEAD_DIM
WOUT_SLAB = HEAD_DIM
_MIX_GMLP = RET_WIDTH
_MIX_XA = RET_WIDTH + GMLP_WIDTH

_V7X_VMEM_LIMIT_BYTES = 60 * 1024 * 1024

_BF16 = jnp.bfloat16
_F32 = jnp.float32


def _dot(a, b):
    return jnp.dot(a, b, preferred_element_type=_F32)


def _dot_nt(a, b):
    return lax.dot_general(a, b, (((1,), (1,)), ((), ())), preferred_element_type=_F32)


def _dot_tn(a, b):
    return lax.dot_general(a, b, (((0,), (0,)), ((), ())), preferred_element_type=_F32)


def _silu(x):
    return x / (1.0 + jnp.exp(-x))


def _rms_scale(x):
    return x * lax.rsqrt(jnp.mean(x * x, axis=-1, keepdims=True) + EPS)


def _center_scale(x):
    mu = jnp.mean(x, axis=-1, keepdims=True)
    d = x - mu
    return d * lax.rsqrt(jnp.mean(d * d, axis=-1, keepdims=True) + EPS)


def _rope(u, cos2, sin2):
    return u * cos2 + pltpu.roll(u, HEAD_DIM // 2, 1) * sin2


def _cols(base, i):
    return slice(base + i * HEAD_DIM, base + (i + 1) * HEAD_DIM)


def _retention_blocks(blocks, dmask_ref, qdec_ref, kdec_ref, sdec_ref):
    sc = [_dot_nt(qb, kb.astype(_BF16)) for qb, kb, _, _, _ in blocks]
    cross = [_dot(qb, state.astype(_BF16)) for qb, _, _, state, _ in blocks]
    new_states = [state * sdec_ref[head] + _dot_tn((kb * kdec_ref[head]).astype(_BF16), vb)
                  for _, kb, vb, state, head in blocks]
    o = [_dot((s * dmask_ref[head]).astype(_BF16), vb) + c * qdec_ref[head]
         for s, c, (_, _, vb, _, head) in zip(sc, cross, blocks)]
    return o, new_states


def _fill_row_bias(dst_ref, bs_ref, n):
    row = lax.broadcasted_iota(jnp.int32, (n, n), 0)
    col = lax.broadcasted_iota(jnp.int32, (n, n), 1)
    for g in range(GMLP_GROUPS):
        diag = jnp.where(row == col, jnp.broadcast_to(bs_ref[g:g + 1, :n], (n, n)), 0.0)
        dst_ref[g] = jnp.broadcast_to(jnp.sum(diag, axis=1, keepdims=True), (n, HEAD_DIM))


def _tril_bf16(w):
    n = w.shape[0]
    row = lax.broadcasted_iota(jnp.int32, (n, n), 0)
    col = lax.broadcasted_iota(jnp.int32, (n, n), 1)
    return jnp.where(row >= col, w, 0.0).astype(_BF16)


def _xattn_heads(heads):
    sc = [_dot_nt(aq.astype(_BF16), mk) * (HEAD_DIM ** -0.5) for aq, mk, _ in heads]
    e = [jnp.exp(s - jnp.max(s, axis=-1, keepdims=True)) for s in sc]
    return [_dot(p.astype(_BF16), mv) / jnp.sum(p, axis=-1, keepdims=True) for p, (_, _, mv) in zip(e, heads)]


def _memkv_kernel(mem_ref, g_ref, w_ref, k_ref, v_ref, k16_ref, v16_ref, w16_s):
    @pl.when(pl.program_id(0) == 0)
    def _():
        w16_s[...] = w_ref[...].astype(_BF16)

    nb = mem_ref.shape[0]
    n = (_rms_scale(mem_ref[...].reshape(nb * MEM_LEN, D_MODEL)) * g_ref[...]).astype(_BF16)
    kv = _dot(n, w16_s[...])
    for i in range(nb):
        k = kv[i * MEM_LEN:(i + 1) * MEM_LEN, :XA_WIDTH]
        v = kv[i * MEM_LEN:(i + 1) * MEM_LEN, XA_WIDTH:]
        for hd in range(XA_HEADS):
            k_ref[i, :, hd, :] = k[:, _cols(0, hd)]
            v_ref[i, :, hd, :] = v[:, _cols(0, hd)]
        k16_ref[i] = k.astype(_BF16)
        v16_ref[i] = v.astype(_BF16)


def _prompt_kernel(x_ref, cos_ref, sin_ref, mk_ref, mv_ref, gnorm_ref, win_ref, gret_ref, ggm_ref, ws_ref, bs_ref,
                   dmask_ref, qdec_ref, kdec_ref, sdec_ref, wout_ref, mix_ref, s_ref, wout16_ref, h_s, bias_s,
                   *, lb, n_out_slabs):
    ts = x_ref.shape[1]
    step = pl.program_id(0) * pl.num_programs(1) + pl.program_id(1)

    @pl.when(step == 0)
    def _():
        _fill_row_bias(bias_s, bs_ref, GMLP_CHUNK)

    @pl.when(step < n_out_slabs)
    def _():
        wout16_ref[...] = wout_ref[...].astype(_BF16)

    @pl.when(pl.program_id(1) == 0)
    def _():
        s_ref[...] = jnp.zeros_like(s_ref)

    x = x_ref[0]
    h_s[...] = (x * (gnorm_ref[...] * _EARLY_OPERAND_SCALE)).astype(_BF16)
    row_scale = lax.rsqrt(jnp.mean(x * x, axis=-1, keepdims=True) + EPS)
    row_scale_b = jnp.broadcast_to(row_scale * (1.0 / _EARLY_OPERAND_SCALE), (ts, HEAD_DIM))

    def proj(c0, width, scaled_rows=True):
        raw = _dot(h_s[...], win_ref[:, c0:c0 + width])
        if scaled_rows:
            return raw
        return jnp.concatenate([raw[:, _cols(0, i)] * row_scale_b for i in range(width // HEAD_DIM)], axis=1)

    cos = cos_ref[...]
    sin = sin_ref[...]

    decay = (dmask_ref, qdec_ref, kdec_ref, sdec_ref)

    def retention_pair(p):
        c0 = 2 * HEAD_DIM * p
        q2, k2, v2, g2 = (proj(sec + c0, 2 * HEAD_DIM, scaled_rows=p > 0) for sec in (_RQ, _RK, _RV, _RG))
        if p == 0:
            h_s[...] = (x_ref[0] * row_scale * gnorm_ref[...]).astype(_BF16)
        heads = (2 * p, 2 * p + 1)
        qh = [_rope(q2[:, _cols(0, hh)], cos, sin).astype(_BF16) for hh in range(2)]
        kh = [_rope(k2[:, _cols(0, hh)], cos, sin) * (HEAD_DIM ** -0.5) for hh in range(2)]
        vh = [v2[:, _cols(0, hh)].astype(_BF16) for hh in range(2)]
        for bi in range(ts // lb):
            rs = slice(bi * lb, (bi + 1) * lb)
            o, new_states = _retention_blocks(
                [(qh[hh][rs], kh[hh][rs], vh[hh][rs], s_ref[0, head], head) for hh, head in enumerate(heads)],
                *decay)
            for hh, head in enumerate(heads):
                s_ref[0, head] = new_states[hh]
                on = _center_scale(o[hh]) * gret_ref[:, _cols(0, head)]
                mix_ref[0, rs, _cols(0, head)] = (on * _silu(g2[rs, _cols(0, hh)])).astype(_BF16)

    def gmlp():
        gvn = _center_scale(proj(_GV, GMLP_WIDTH)) * ggm_ref[...]
        gg = proj(_GG, GMLP_WIDTH)
        gu = proj(_GU, GMLP_WIDTH)
        for g in range(GMLP_GROUPS):
            hs = _cols(0, g)
            wm = _tril_bf16(ws_ref[g])
            for c in range(ts // GMLP_CHUNK):
                rs = slice(c * GMLP_CHUNK, (c + 1) * GMLP_CHUNK)
                sg = _dot(wm, gvn[rs, hs].astype(_BF16)) + bias_s[g]
                mix_ref[0, rs, _cols(_MIX_GMLP, g)] = (gu[rs, hs] * (sg * _silu(gg[rs, hs]))).astype(_BF16)

    def xattn():
        aq = proj(_AQ, XA_WIDTH)
        ag = proj(_AG, XA_WIDTH)
        ao = _xattn_heads([(aq[:, _cols(0, hd)], mk_ref[0, :, _cols(0, hd)], mv_ref[0, :, _cols(0, hd)])
                           for hd in range(XA_HEADS)])
        for hd in range(XA_HEADS):
            mix_ref[0, :, _cols(_MIX_XA, hd)] = (ao[hd] * _silu(ag[:, _cols(0, hd)])).astype(_BF16)

    for p in range(RET_HEADS // 2):
        retention_pair(p)
    xattn()
    gmlp()


class _SlabView:
    def __init__(self, ref):
        self.ref = ref

    def _at(self, rows, base, i):
        col = base + i * HEAD_DIM
        return (col // W_SLAB, rows, slice(col % W_SLAB, col % W_SLAB + HEAD_DIM))

    def get(self, rows, base, i):
        return self.ref[self._at(rows, base, i)]

    def set(self, rows, base, i, value):
        self.ref[self._at(rows, base, i)] = value


def _sample_kernel(x_ref, cos_ref, sin_ref, ck_hbm, cv_hbm, s0_ref, gnorm_ref, w_ref, gret_ref, ggm_ref, ws_ref,
                   bs_ref, dmask_ref, qdec_ref, kdec_ref, sdec_ref, w16_ref, mix_ref, s_ref, gvn_ref, h_s, proj_s,
                   bias_s, kbuf, vbuf, kv_sem, *, ts, sp, n_slabs, n_branch):
    n = pl.program_id(0)
    proj = _SlabView(proj_s)
    all_rows = slice(None)

    def kv_copies(j, slot):
        copies = []
        for i in range(sp):
            for hd in range(XA_HEADS):
                for a, (src, dst) in enumerate(((ck_hbm, kbuf), (cv_hbm, vbuf))):
                    copies.append(pltpu.make_async_copy(
                        src.at[j * sp + i, :, hd, :], dst.at[slot, i * XA_HEADS + hd], kv_sem.at[slot, a]))
        return copies

    @pl.when(n == 0)
    def _():
        for copy in kv_copies(0, 0):
            copy.start()
        h_s[...] = (_rms_scale(x_ref[...]) * gnorm_ref[...]).astype(_BF16)
        _fill_row_bias(bias_s, bs_ref, ts)

    @pl.when(n < n_slabs)
    def _():
        w16 = w_ref[...].astype(_BF16)
        w16_ref[...] = w16
        proj_s[n] = _dot(h_s[...], w16)

    @pl.when(n == n_slabs - 1)
    def _():
        cos = cos_ref[...]
        sin = sin_ref[...]
        for head in range(RET_HEADS):
            proj.set(all_rows, _RQ, head, _rope(proj.get(all_rows, _RQ, head), cos, sin))
            proj.set(all_rows, _RK, head, _rope(proj.get(all_rows, _RK, head), cos, sin) * (HEAD_DIM ** -0.5))
        gv = jnp.concatenate([proj.get(all_rows, _GV, g) for g in range(GMLP_GROUPS)], axis=1)
        gvn = _center_scale(gv) * ggm_ref[...]
        gvn_ref[...] = gvn
        for g in range(GMLP_GROUPS):
            proj.set(all_rows, _GV, g, gvn[:, _cols(0, g)])

    @pl.when(n >= n_slabs)
    def _():
        j = n - n_slabs
        slot = j % 2
        rows = [pl.ds(pl.multiple_of((j * sp + i) * ts, ts), ts) for i in range(sp)]

        @pl.when(j + 1 < n_branch)
        def _():
            for copy in kv_copies(j + 1, 1 - slot):
                copy.start()

        for copy in kv_copies(j, slot):
            copy.wait()

        units = [(i, head) for i in range(sp) for head in range(RET_HEADS)]
        o, new_states = _retention_blocks(
            [(proj.get(rows[i], _RQ, head).astype(_BF16), proj.get(rows[i], _RK, head),
              proj.get(rows[i], _RV, head).astype(_BF16), s0_ref[i, head], head) for i, head in units],
            dmask_ref, qdec_ref, kdec_ref, sdec_ref)
        for (i, head), o_u, s_u in zip(units, o, new_states):
            s_ref[i, head] = s_u
            on = _center_scale(o_u) * gret_ref[:, _cols(0, head)]
            mix_ref[rows[i], _cols(0, head)] = (on * _silu(proj.get(rows[i], _RG, head))).astype(_BF16)

        for g in range(GMLP_GROUPS):
            wm = _tril_bf16(ws_ref[g, :ts, :ts])
            for i in range(sp):
                sg = _dot(wm, proj.get(rows[i], _GV, g).astype(_BF16)) + bias_s[g]
                mix_ref[rows[i], _cols(_MIX_GMLP, g)] = (
                    proj.get(rows[i], _GU, g) * sg * _silu(proj.get(rows[i], _GG, g))).astype(_BF16)

        units = [(i, hd) for i in range(sp) for hd in range(XA_HEADS)]
        ao = _xattn_heads([(proj.get(rows[i], _AQ, hd), kbuf[slot, i * XA_HEADS + hd].astype(_BF16),
                            vbuf[slot, i * XA_HEADS + hd].astype(_BF16)) for i, hd in units])
        for (i, hd), ao_u in zip(units, ao):
            mix_ref[rows[i], _cols(_MIX_XA, hd)] = (ao_u * _silu(proj.get(rows[i], _AG, hd))).astype(_BF16)


def _outproj_kernel(xp_ref, mixp_ref, xs_hbm, mixs_hbm, w_ref, g_ref, yp_ref, ys_ref, xs_buf, mixs_buf, s_sem,
                    *, n_prompt_tiles):
    n = pl.program_id(0)
    sample_copies = (pltpu.make_async_copy(xs_hbm, xs_buf, s_sem.at[0]),
                     pltpu.make_async_copy(mixs_hbm, mixs_buf, s_sem.at[1]))

    def project(x_ref, mix_ref, y_ref):
        y = x_ref[...] + _dot(mix_ref[...], w_ref[...])
        y_ref[...] = _rms_scale(y) * g_ref[...]

    @pl.when(n == 0)
    def _():
        for copy in sample_copies:
            copy.start()

    @pl.when(n < n_prompt_tiles)
    def _():
        project(xp_ref, mixp_ref, yp_ref)

    @pl.when(n == n_prompt_tiles)
    def _():
        for copy in sample_copies:
            copy.wait()
        project(xs_buf, mixs_buf, ys_ref)


def _const_spec(shape):
    return pl.BlockSpec(shape, lambda *_: (0,) * len(shape), pipeline_mode=pl.Buffered(1))


def _rope_tables(pos):
    inv_freq = ROPE_BASE ** (-np.arange(0, HEAD_DIM, 2, dtype=np.float64) / HEAD_DIM)
    ang = np.asarray(pos, np.float64)[:, None] * inv_freq[None, :]
    cos, sin = np.cos(ang), np.sin(ang)
    return (np.concatenate([cos, cos], axis=-1).astype(np.float32),
            np.concatenate([-sin, sin], axis=-1).astype(np.float32))


def _decay_tables(lb):
    log_gamma = np.log(1.0 - 2.0 ** (-5.0 - np.arange(RET_HEADS, dtype=np.float64)))
    idx = np.arange(lb, dtype=np.float64)
    diff = idx[:, None] - idx[None, :]
    dmask = np.where(diff >= 0, np.exp(log_gamma[:, None, None] * np.maximum(diff, 0.0)[None]), 0.0)
    qdec = np.exp(log_gamma[:, None] * (idx[None, :] + 1.0))
    kdec = np.exp(log_gamma[:, None] * (lb - 1.0 - idx[None, :]))
    sdec = np.exp(log_gamma * lb)
    bcast = lambda a: np.broadcast_to(a[..., None], a.shape + (HEAD_DIM,)).astype(np.float32)
    return [dmask.astype(np.float32), bcast(qdec), bcast(kdec), bcast(sdec)[:, None, :]]


def _branch_consts(weights, lb):
    g_ret, g_gmlp, w_s, b_s = weights
    return [g_ret.reshape(1, -1), g_gmlp.reshape(1, -1), w_s, b_s] + _decay_tables(lb)


def _layer_prompt(x, mk16, mv16, g_norm, w_in16, w_out, weights, *, ts, lb):
    nb, l, _ = x.shape
    nt = l // ts
    n_out_slabs = D_MODEL // WOUT_SLAB
    assert n_out_slabs <= nb * nt
    tile = lambda width: pl.BlockSpec((1, ts, width), lambda b, t: (b, t, 0))
    per_stream = lambda shape: pl.BlockSpec(shape, lambda b, t: (b,) + (0,) * (len(shape) - 1))
    rope_spec = pl.BlockSpec((ts, HEAD_DIM), lambda b, t: (t, 0))
    out_slab_spec = pl.BlockSpec((MIX_WIDTH, WOUT_SLAB), lambda b, t: (0, jnp.minimum(b * nt + t, n_out_slabs - 1)))
    state_shape = (1, RET_HEADS, HEAD_DIM, HEAD_DIM)
    kv_spec = per_stream((1, MEM_LEN, XA_WIDTH))
    consts = [g_norm.reshape(1, -1), w_in16] + _branch_consts(weights, lb)
    return pl.pallas_call(
        functools.partial(_prompt_kernel, lb=lb, n_out_slabs=n_out_slabs),
        grid=(nb, nt),
        in_specs=[tile(D_MODEL), rope_spec, rope_spec, kv_spec, kv_spec] + [_const_spec(c.shape) for c in consts]
        + [out_slab_spec],
        out_specs=[tile(MIX_WIDTH), per_stream(state_shape), out_slab_spec],
        out_shape=[jax.ShapeDtypeStruct((nb, l, MIX_WIDTH), _BF16),
                   jax.ShapeDtypeStruct((nb,) + state_shape[1:], _F32),
                   jax.ShapeDtypeStruct(w_out.shape, _BF16)],
        scratch_shapes=[pltpu.VMEM((ts, D_MODEL), _BF16), pltpu.VMEM((GMLP_GROUPS, GMLP_CHUNK, HEAD_DIM), _F32)],
        compiler_params=pltpu.CompilerParams(
            dimension_semantics=("arbitrary", "arbitrary"), vmem_limit_bytes=_V7X_VMEM_LIMIT_BYTES),
        name="layer_prompt",
    )(x, *_rope_tables(np.arange(l)), mk16, mv16, *consts, w_out)


def _layer_sample(x, cache_k, cache_v, s0, g_norm, w_in, weights, *, sp):
    n, ts, _ = x.shape
    m = n * ts
    n_slabs = IN_WIDTH // W_SLAB
    branch_step = lambda i: jnp.maximum(i - n_slabs, 0)
    per_step = lambda shape: pl.BlockSpec(shape, lambda i: (branch_step(i),) + (0,) * (len(shape) - 1))
    slab_spec = pl.BlockSpec((D_MODEL, W_SLAB), lambda i: (0, jnp.minimum(i, n_slabs - 1)))
    state_shape = (sp, RET_HEADS, HEAD_DIM, HEAD_DIM)
    kv_spec = pl.BlockSpec(memory_space=pl.ANY)
    kv_buf = pltpu.VMEM((2, sp * XA_HEADS, MEM_LEN, HEAD_DIM), _F32)
    consts = _branch_consts(weights, ts)
    w_in16, mix, s_new, gvn = pl.pallas_call(
        functools.partial(_sample_kernel, ts=ts, sp=sp, n_slabs=n_slabs, n_branch=n // sp),
        grid=(n_slabs + n // sp,),
        in_specs=[_const_spec((m, D_MODEL)), _const_spec((m, HEAD_DIM)), _const_spec((m, HEAD_DIM)), kv_spec, kv_spec,
                  per_step(state_shape), _const_spec((1, D_MODEL)), slab_spec]
        + [_const_spec(c.shape) for c in consts],
        out_specs=[slab_spec, pl.BlockSpec((m, MIX_WIDTH), lambda i: (0, 0)), per_step(state_shape),
                   pl.BlockSpec((m, GMLP_WIDTH), lambda i: (0, 0))],
        out_shape=[jax.ShapeDtypeStruct(w_in.shape, _BF16),
                   jax.ShapeDtypeStruct((m, MIX_WIDTH), _BF16),
                   jax.ShapeDtypeStruct((n,) + state_shape[1:], _F32),
                   jax.ShapeDtypeStruct((m, GMLP_WIDTH), _F32)],
        scratch_shapes=[pltpu.VMEM((m, D_MODEL), _BF16), pltpu.VMEM((n_slabs, m, W_SLAB), _F32),
                        pltpu.VMEM((GMLP_GROUPS, ts, HEAD_DIM), _F32), kv_buf, kv_buf,
                        pltpu.SemaphoreType.DMA((2, 2))],
        compiler_params=pltpu.CompilerParams(
            dimension_semantics=("arbitrary",), vmem_limit_bytes=_V7X_VMEM_LIMIT_BYTES),
        name="layer_sample",
    )(x.reshape(m, D_MODEL), *_rope_tables(np.tile(PAST_LEN + np.arange(ts), n)), cache_k, cache_v, s0,
      g_norm.reshape(1, -1), w_in, *consts)
    return w_in16, mix, s_new, gvn.reshape(n, ts, GMLP_WIDTH)


def _out_proj(x_p, mix_p, x_s, mix_s, w_out16, g_final, *, tile_rows):
    n_p = x_p.shape[0] // tile_rows
    p_blk = pl.BlockSpec((tile_rows, D_MODEL), lambda n: (jnp.minimum(n, n_p - 1), 0))
    s_out_blk = pl.BlockSpec(x_s.shape, lambda n: (0, 0))
    return pl.pallas_call(
        functools.partial(_outproj_kernel, n_prompt_tiles=n_p),
        grid=(n_p + 1,),
        in_specs=[p_blk, p_blk, pl.BlockSpec(memory_space=pl.ANY), pl.BlockSpec(memory_space=pl.ANY),
                  _const_spec(w_out16.shape), _const_spec((1, D_MODEL))],
        out_specs=[p_blk, s_out_blk],
        out_shape=[jax.ShapeDtypeStruct(x_p.shape, _F32), jax.ShapeDtypeStruct(x_s.shape, _F32)],
        scratch_shapes=[pltpu.VMEM(x_s.shape, _F32), pltpu.VMEM(mix_s.shape, _BF16), pltpu.SemaphoreType.DMA((2,))],
        compiler_params=pltpu.CompilerParams(
            dimension_semantics=("arbitrary",), vmem_limit_bytes=_V7X_VMEM_LIMIT_BYTES),
        name="out_proj",
    )(x_p, mix_p, x_s, mix_s, w_out16, g_final.reshape(1, -1))


def _memory_kv(mem, g_mem, w_mem_kv):
    b = mem.shape[0]
    nb = MEMKV_STREAMS
    blk = lambda width: pl.BlockSpec((nb, MEM_LEN, width), lambda i: (i, 0, 0))
    blk4 = pl.BlockSpec((nb, MEM_LEN, XA_HEADS, HEAD_DIM), lambda i: (i, 0, 0, 0))
    kv_f32 = jax.ShapeDtypeStruct((b, MEM_LEN, XA_HEADS, HEAD_DIM), _F32)
    kv_b16 = jax.ShapeDtypeStruct((b, MEM_LEN, XA_WIDTH), _BF16)
    return pl.pallas_call(
        _memkv_kernel,
        grid=(b // nb,),
        in_specs=[blk(D_MODEL), _const_spec((1, D_MODEL)), _const_spec(w_mem_kv.shape)],
        out_specs=[blk4, blk4, blk(XA_WIDTH), blk(XA_WIDTH)],
        out_shape=[kv_f32, kv_f32, kv_b16, kv_b16],
        scratch_shapes=[pltpu.VMEM(w_mem_kv.shape, _BF16)],
        compiler_params=pltpu.CompilerParams(dimension_semantics=("arbitrary",)),
        name="memory_kv",
    )(mem, g_mem.reshape(1, -1), w_mem_kv)


MEMKV_STREAMS = 2
PROMPT_TILE = 512
RET_BLOCK = 256
OUT_TILE = 512
SAMPLE_STEP = 4


def kernel(x_prompt, x_sample, mem_prompt, state_ret, cache_mem_k, cache_mem_v, g_norm, w_in, g_ret, g_gmlp,
           w_s, b_s, g_mem, w_mem_kv, w_out, g_final):
    assert g_norm.shape[0] == 1
    b_p, l_p, _ = x_prompt.shape
    b_s_, l_s, _ = x_sample.shape
    weights = (g_ret[0], g_gmlp[0], w_s[0], b_s[0])

    mk, mv, mk16, mv16 = _memory_kv(mem_prompt, g_mem[0], w_mem_kv[0])

    w_in16, mix_s, s_s, gvn_s = _layer_sample(
        x_sample, cache_mem_k[0], cache_mem_v[0], state_ret[0], g_norm[0], w_in[0], weights, sp=SAMPLE_STEP)

    mix_p, s_p, w_out16 = _layer_prompt(x_prompt, mk16, mv16, g_norm[0], w_in16, w_out[0], weights,
                                        ts=PROMPT_TILE, lb=RET_BLOCK)
    y_p, y_s = _out_proj(x_prompt.reshape(b_p * l_p, D_MODEL), mix_p.reshape(b_p * l_p, MIX_WIDTH),
                         x_sample.reshape(b_s_ * l_s, D_MODEL), mix_s, w_out16, g_final, tile_rows=OUT_TILE)

    return (y_p.reshape(x_prompt.shape), y_s.reshape(x_sample.shape), s_p[None], mk[None], mv[None],
            s_s[None], gvn_s[None])
```

```python
import functools

import numpy as np

import jax
import jax.numpy as jnp
from jax import lax
from jax.experimental import pallas as pl
from jax.experimental.pallas import tpu as pltpu

D_MODEL = 2048
PAST_LEN = 1024
MEM_LEN = 256
RET_HEADS = 8
HEAD_DIM = 128
RET_WIDTH = RET_HEADS * HEAD_DIM
GMLP_GROUPS = 4
GMLP_WIDTH = GMLP_GROUPS * HEAD_DIM
GMLP_CHUNK = 128
XA_HEADS = 4
XA_WIDTH = XA_HEADS * HEAD_DIM
MIX_WIDTH = RET_WIDTH + GMLP_WIDTH + XA_WIDTH
ROPE_BASE = 10000.0
EPS = 1e-6
_EARLY_OPERAND_SCALE = 2.0 ** -32

_RQ, _RK, _RV, _RG = 0, RET_WIDTH, 2 * RET_WIDTH, 3 * RET_WIDTH
_GU = 4 * RET_WIDTH
_GV = _GU + GMLP_WIDTH
_GG = _GV + GMLP_WIDTH
_AQ = _GG + GMLP_WIDTH
_AG = _AQ + XA_WIDTH
IN_WIDTH = _AG + XA_WIDTH
W_SLAB = 4 * HEAD_DIM
WOUT_SLAB = HEAD_DIM
_MIX_GMLP = RET_WIDTH
_MIX_XA = RET_WIDTH + GMLP_WIDTH

_V7X_VMEM_LIMIT_BYTES = 60 * 1024 * 1024

_BF16 = jnp.bfloat16
_F32 = jnp.float32


def _dot(a, b):
    return jnp.dot(a, b, preferred_element_type=_F32)


def _dot_nt(a, b):
    return lax.dot_general(a, b, (((1,), (1,)), ((), ())), preferred_element_type=_F32)


def _dot_tn(a, b):
    return lax.dot_general(a, b, (((0,), (0,)), ((), ())), preferred_element_type=_F32)


def _silu(x):
    return x / (1.0 + jnp.exp(-x))


def _rms_scale(x):
    return x * lax.rsqrt(jnp.mean(x * x, axis=-1, keepdims=True) + EPS)


def _center_scale(x):
    mu = jnp.mean(x, axis=-1, keepdims=True)
    d = x - mu
    return d * lax.rsqrt(jnp.mean(d * d, axis=-1, keepdims=True) + EPS)


def _rope(u, cos2, sin2):
    return u * cos2 + pltpu.roll(u, HEAD_DIM // 2, 1) * sin2


def _cols(base, i):
    return slice(base + i * HEAD_DIM, base + (i + 1) * HEAD_DIM)


def _retention_blocks(blocks, dmask_ref, qdec_ref, kdec_ref, sdec_ref):
    sc = [_dot_nt(qb, kb.astype(_BF16)) for qb, kb, _, _, _ in blocks]
    cross = [_dot(qb, state.astype(_BF16)) for qb, _, _, state, _ in blocks]
    new_states = [state * sdec_ref[head] + _dot_tn((kb * kdec_ref[head]).astype(_BF16), vb)
                  for _, kb, vb, state, head in blocks]
    o = [_dot((s * dmask_ref[head]).astype(_BF16), vb) + c * qdec_ref[head]
         for s, c, (_, _, vb, _, head) in zip(sc, cross, blocks)]
    return o, new_states


def _fill_row_bias(dst_ref, bs_ref, n):
    row = lax.broadcasted_iota(jnp.int32, (n, n), 0)
    col = lax.broadcasted_iota(jnp.int32, (n, n), 1)
    for g in range(GMLP_GROUPS):
        diag = jnp.where(row == col, jnp.broadcast_to(bs_ref[g:g + 1, :n], (n, n)), 0.0)
        dst_ref[g] = jnp.broadcast_to(jnp.sum(diag, axis=1, keepdims=True), (n, HEAD_DIM))


def _tril_bf16(w):
    n = w.shape[0]
    row = lax.broadcasted_iota(jnp.int32, (n, n), 0)
    col = lax.broadcasted_iota(jnp.int32, (n, n), 1)
    return jnp.where(row >= col, w, 0.0).astype(_BF16)


def _xattn_heads(heads):
    sc = [_dot_nt(aq.astype(_BF16), mk) * (HEAD_DIM ** -0.5) for aq, mk, _ in heads]
    e = [jnp.exp(s - jnp.max(s, axis=-1, keepdims=True)) for s in sc]
    return [_dot(p.astype(_BF16), mv) / jnp.sum(p, axis=-1, keepdims=True) for p, (_, _, mv) in zip(e, heads)]


def _memkv_kernel(mem_ref, g_ref, w_hbm, k_ref, v_ref, k16_ref, v16_ref, wf_s, w16_s, w_sem):
    nb = mem_ref.shape[0]
    n_chunks = 2 * XA_WIDTH // MEMKV_CHUNK
    heads_per_chunk = MEMKV_CHUNK // HEAD_DIM
    chunk = lambda c: slice(c * MEMKV_CHUNK, (c + 1) * MEMKV_CHUNK)
    copies = [pltpu.make_async_copy(w_hbm.at[:, chunk(c)], wf_s.at[:, chunk(c)], w_sem.at[c]) for c in range(n_chunks)]

    def body(first):
        if first:
            for copy in copies:
                copy.start()
        n = (_rms_scale(mem_ref[...].reshape(nb * MEM_LEN, D_MODEL)) * g_ref[...]).astype(_BF16)
        for c in range(n_chunks):
            if first:
                copies[c].wait()
                w16_s[:, chunk(c)] = wf_s[:, chunk(c)].astype(_BF16)
            kvc = _dot(n, w16_s[:, chunk(c)])
            is_k = c < n_chunks // 2
            f32_ref, b16_ref = (k_ref, k16_ref) if is_k else (v_ref, v16_ref)
            c0 = (c if is_k else c - n_chunks // 2) * MEMKV_CHUNK
            for i in range(nb):
                rows = kvc[i * MEM_LEN:(i + 1) * MEM_LEN]
                for h in range(heads_per_chunk):
                    f32_ref[i, :, c0 // HEAD_DIM + h, :] = rows[:, _cols(0, h)]
                b16_ref[i, :, c0:c0 + MEMKV_CHUNK] = rows.astype(_BF16)

    @pl.when(pl.program_id(0) == 0)
    def _():
        body(True)

    @pl.when(pl.program_id(0) > 0)
    def _():
        body(False)


def _prompt_kernel(x_ref, cos_ref, sin_ref, mk_ref, mv_ref, gnorm_ref, win_ref, gret_ref, ggm_ref, ws_ref, bs_ref,
                   dmask_ref, qdec_ref, kdec_ref, sdec_ref, wout_ref, mix_ref, s_ref, wout16_ref, h_s, bias_s,
                   *, lb, n_out_slabs):
    ts = x_ref.shape[1]
    step = pl.program_id(0) * pl.num_programs(1) + pl.program_id(1)

    @pl.when(step == 0)
    def _():
        _fill_row_bias(bias_s, bs_ref, GMLP_CHUNK)

    @pl.when(step < n_out_slabs)
    def _():
        wout16_ref[...] = wout_ref[...].astype(_BF16)

    @pl.when(pl.program_id(1) == 0)
    def _():
        s_ref[...] = jnp.zeros_like(s_ref)

    x = x_ref[0]
    h_s[...] = (x * (gnorm_ref[...] * _EARLY_OPERAND_SCALE)).astype(_BF16)
    row_scale = lax.rsqrt(jnp.mean(x * x, axis=-1, keepdims=True) + EPS)
    row_scale_b = jnp.broadcast_to(row_scale * (1.0 / _EARLY_OPERAND_SCALE), (ts, HEAD_DIM))

    def proj(c0, width, scaled_rows=True):
        raw = _dot(h_s[...], win_ref[:, c0:c0 + width])
        if scaled_rows:
            return raw
        return jnp.concatenate([raw[:, _cols(0, i)] * row_scale_b for i in range(width // HEAD_DIM)], axis=1)

    cos = cos_ref[...]
    sin = sin_ref[...]

    decay = (dmask_ref, qdec_ref, kdec_ref, sdec_ref)

    def retention_pair(p):
        c0 = 2 * HEAD_DIM * p
        q2, k2, v2, g2 = (proj(sec + c0, 2 * HEAD_DIM, scaled_rows=p > 0) for sec in (_RQ, _RK, _RV, _RG))
        if p == 0:
            h_s[...] = (x_ref[0] * row_scale * gnorm_ref[...]).astype(_BF16)
        heads = (2 * p, 2 * p + 1)
        qh = [_rope(q2[:, _cols(0, hh)], cos, sin).astype(_BF16) for hh in range(2)]
        kh = [_rope(k2[:, _cols(0, hh)], cos, sin) * (HEAD_DIM ** -0.5) for hh in range(2)]
        vh = [v2[:, _cols(0, hh)].astype(_BF16) for hh in range(2)]
        for bi in range(ts // lb):
            rs = slice(bi * lb, (bi + 1) * lb)
            o, new_states = _retention_blocks(
                [(qh[hh][rs], kh[hh][rs], vh[hh][rs], s_ref[0, head], head) for hh, head in enumerate(heads)],
                *decay)
            for hh, head in enumerate(heads):
                s_ref[0, head] = new_states[hh]
                on = _center_scale(o[hh]) * gret_ref[:, _cols(0, head)]
                mix_ref[0, rs, _cols(0, head)] = (on * _silu(g2[rs, _cols(0, hh)])).astype(_BF16)

    def gmlp():
        gvn = _center_scale(proj(_GV, GMLP_WIDTH)) * ggm_ref[...]
        gg = proj(_GG, GMLP_WIDTH)
        gu = proj(_GU, GMLP_WIDTH)
        for g in range(GMLP_GROUPS):
            hs = _cols(0, g)
            wm = _tril_bf16(ws_ref[g])
            for c in range(ts // GMLP_CHUNK):
                rs = slice(c * GMLP_CHUNK, (c + 1) * GMLP_CHUNK)
                sg = _dot(wm, gvn[rs, hs].astype(_BF16)) + bias_s[g]
                mix_ref[0, rs, _cols(_MIX_GMLP, g)] = (gu[rs, hs] * (sg * _silu(gg[rs, hs]))).astype(_BF16)

    def xattn():
        aq = proj(_AQ, XA_WIDTH)
        ag = proj(_AG, XA_WIDTH)
        ao = _xattn_heads([(aq[:, _cols(0, hd)], mk_ref[0, :, _cols(0, hd)], mv_ref[0, :, _cols(0, hd)])
                           for hd in range(XA_HEADS)])
        for hd in range(XA_HEADS):
            mix_ref[0, :, _cols(_MIX_XA, hd)] = (ao[hd] * _silu(ag[:, _cols(0, hd)])).astype(_BF16)

    for p in range(RET_HEADS // 2):
        retention_pair(p)
    xattn()
    gmlp()


class _SlabView:
    def __init__(self, ref):
        self.ref = ref

    def _at(self, rows, base, i):
        col = base + i * HEAD_DIM
        return (col // W_SLAB, rows, slice(col % W_SLAB, col % W_SLAB + HEAD_DIM))

    def get(self, rows, base, i):
        return self.ref[self._at(rows, base, i)]

    def set(self, rows, base, i, value):
        self.ref[self._at(rows, base, i)] = value


def _sample_kernel(x_ref, cos_ref, sin_ref, ck_hbm, cv_hbm, s0_ref, gnorm_ref, w_ref, gret_ref, ggm_ref, ws_ref,
                   bs_ref, dmask_ref, qdec_ref, kdec_ref, sdec_ref, w16_ref, mix_ref, s_ref, gvn_ref, h_s, proj_s,
                   bias_s, kbuf, vbuf, kv_sem, *, ts, sp, n_slabs, n_branch):
    n = pl.program_id(0)
    proj = _SlabView(proj_s)
    all_rows = slice(None)

    def kv_copies(j, slot):
        copies = []
        for i in range(sp):
            for hd in range(XA_HEADS):
                for a, (src, dst) in enumerate(((ck_hbm, kbuf), (cv_hbm, vbuf))):
                    copies.append(pltpu.make_async_copy(
                        src.at[j * sp + i, :, hd, :], dst.at[slot, i * XA_HEADS + hd], kv_sem.at[slot, a]))
        return copies

    @pl.when(n == 0)
    def _():
        for copy in kv_copies(0, 0):
            copy.start()
        h_s[...] = (_rms_scale(x_ref[...]) * gnorm_ref[...]).astype(_BF16)
        _fill_row_bias(bias_s, bs_ref, ts)

    @pl.when(n < n_slabs)
    def _():
        w16 = w_ref[...].astype(_BF16)
        w16_ref[...] = w16
        proj_s[n] = _dot(h_s[...], w16)

    @pl.when(n == n_slabs - 1)
    def _():
        cos = cos_ref[...]
        sin = sin_ref[...]
        for head in range(RET_HEADS):
            proj.set(all_rows, _RQ, head, _rope(proj.get(all_rows, _RQ, head), cos, sin))
            proj.set(all_rows, _RK, head, _rope(proj.get(all_rows, _RK, head), cos, sin) * (HEAD_DIM ** -0.5))
        gv = jnp.concatenate([proj.get(all_rows, _GV, g) for g in range(GMLP_GROUPS)], axis=1)
        gvn = _center_scale(gv) * ggm_ref[...]
        gvn_ref[...] = gvn
        for g in range(GMLP_GROUPS):
            proj.set(all_rows, _GV, g, gvn[:, _cols(0, g)])

    @pl.when(n >= n_slabs)
    def _():
        j = n - n_slabs
        slot = j % 2
        rows = [pl.ds(pl.multiple_of((j * sp + i) * ts, ts), ts) for i in range(sp)]

        @pl.when(j + 1 < n_branch)
        def _():
            for copy in kv_copies(j + 1, 1 - slot):
                copy.start()

        for copy in kv_copies(j, slot):
            copy.wait()

        units = [(i, head) for i in range(sp) for head in range(RET_HEADS)]
        o, new_states = _retention_blocks(
            [(proj.get(rows[i], _RQ, head).astype(_BF16), proj.get(rows[i], _RK, head),
              proj.get(rows[i], _RV, head).astype(_BF16), s0_ref[i, head], head) for i, head in units],
            dmask_ref, qdec_ref, kdec_ref, sdec_ref)
        for (i, head), o_u, s_u in zip(units, o, new_states):
            s_ref[i, head] = s_u
            on = _center_scale(o_u) * gret_ref[:, _cols(0, head)]
            mix_ref[rows[i], _cols(0, head)] = (on * _silu(proj.get(rows[i], _RG, head))).astype(_BF16)

        for g in range(GMLP_GROUPS):
            wm = _tril_bf16(ws_ref[g, :ts, :ts])
            for i in range(sp):
                sg = _dot(wm, proj.get(rows[i], _GV, g).astype(_BF16)) + bias_s[g]
                mix_ref[rows[i], _cols(_MIX_GMLP, g)] = (
                    proj.get(rows[i], _GU, g) * sg * _silu(proj.get(rows[i], _GG, g))).astype(_BF16)

        units = [(i, hd) for i in range(sp) for hd in range(XA_HEADS)]
        ao = _xattn_heads([(proj.get(rows[i], _AQ, hd), kbuf[slot, i * XA_HEADS + hd].astype(_BF16),
                            vbuf[slot, i * XA_HEADS + hd].astype(_BF16)) for i, hd in units])
        for (i, hd), ao_u in zip(units, ao):
            mix_ref[rows[i], _cols(_MIX_XA, hd)] = (ao_u * _silu(proj.get(rows[i], _AG, hd))).astype(_BF16)


def _outproj_kernel(xp_ref, mixp_ref, xs_hbm, mixs_hbm, w_ref, g_ref, yp_ref, ys_ref, xs_buf, mixs_buf, s_sem,
                    *, n_prompt_tiles):
    n = pl.program_id(0)
    sample_copies = (pltpu.make_async_copy(xs_hbm, xs_buf, s_sem.at[0]),
                     pltpu.make_async_copy(mixs_hbm, mixs_buf, s_sem.at[1]))

    def project(x_ref, mix_ref, y_ref):
        y = x_ref[...] + _dot(mix_ref[...], w_ref[...])
        y_ref[...] = _rms_scale(y) * g_ref[...]

    @pl.when(n == 0)
    def _():
        for copy in sample_copies:
            copy.start()

    @pl.when(n < n_prompt_tiles)
    def _():
        project(xp_ref, mixp_ref, yp_ref)

    @pl.when(n == n_prompt_tiles)
    def _():
        for copy in sample_copies:
            copy.wait()
        project(xs_buf, mixs_buf, ys_ref)


def _const_spec(shape):
    return pl.BlockSpec(shape, lambda *_: (0,) * len(shape), pipeline_mode=pl.Buffered(1))


def _rope_tables(pos):
    inv_freq = ROPE_BASE ** (-np.arange(0, HEAD_DIM, 2, dtype=np.float64) / HEAD_DIM)
    ang = np.asarray(pos, np.float64)[:, None] * inv_freq[None, :]
    cos, sin = np.cos(ang), np.sin(ang)
    return (np.concatenate([cos, cos], axis=-1).astype(np.float32),
            np.concatenate([-sin, sin], axis=-1).astype(np.float32))


def _decay_tables(lb):
    log_gamma = np.log(1.0 - 2.0 ** (-5.0 - np.arange(RET_HEADS, dtype=np.float64)))
    idx = np.arange(lb, dtype=np.float64)
    diff = idx[:, None] - idx[None, :]
    dmask = np.where(diff >= 0, np.exp(log_gamma[:, None, None] * np.maximum(diff, 0.0)[None]), 0.0)
    qdec = np.exp(log_gamma[:, None] * (idx[None, :] + 1.0))
    kdec = np.exp(log_gamma[:, None] * (lb - 1.0 - idx[None, :]))
    sdec = np.exp(log_gamma * lb)
    bcast = lambda a: np.broadcast_to(a[..., None], a.shape + (HEAD_DIM,)).astype(np.float32)
    return [dmask.astype(np.float32), bcast(qdec), bcast(kdec), bcast(sdec)[:, None, :]]


def _branch_consts(weights, lb):
    g_ret, g_gmlp, w_s, b_s = weights
    return [g_ret.reshape(1, -1), g_gmlp.reshape(1, -1), w_s, b_s] + _decay_tables(lb)


def _layer_prompt(x, mk16, mv16, g_norm, w_in16, w_out, weights, *, ts, lb):
    nb, l, _ = x.shape
    nt = l // ts
    n_out_slabs = D_MODEL // WOUT_SLAB
    assert n_out_slabs <= nb * nt
    tile = lambda width: pl.BlockSpec((1, ts, width), lambda b, t: (b, t, 0))
    per_stream = lambda shape: pl.BlockSpec(shape, lambda b, t: (b,) + (0,) * (len(shape) - 1))
    rope_spec = pl.BlockSpec((ts, HEAD_DIM), lambda b, t: (t, 0))
    out_slab_spec = pl.BlockSpec((MIX_WIDTH, WOUT_SLAB), lambda b, t: (0, jnp.minimum(b * nt + t, n_out_slabs - 1)))
    state_shape = (1, RET_HEADS, HEAD_DIM, HEAD_DIM)
    kv_spec = per_stream((1, MEM_LEN, XA_WIDTH))
    consts = [g_norm.reshape(1, -1), w_in16] + _branch_consts(weights, lb)
    return pl.pallas_call(
        functools.partial(_prompt_kernel, lb=lb, n_out_slabs=n_out_slabs),
        grid=(nb, nt),
        in_specs=[tile(D_MODEL), rope_spec, rope_spec, kv_spec, kv_spec] + [_const_spec(c.shape) for c in consts]
        + [out_slab_spec],
        out_specs=[tile(MIX_WIDTH), per_stream(state_shape), out_slab_spec],
        out_shape=[jax.ShapeDtypeStruct((nb, l, MIX_WIDTH), _BF16),
                   jax.ShapeDtypeStruct((nb,) + state_shape[1:], _F32),
                   jax.ShapeDtypeStruct(w_out.shape, _BF16)],
        scratch_shapes=[pltpu.VMEM((ts, D_MODEL), _BF16), pltpu.VMEM((GMLP_GROUPS, GMLP_CHUNK, HEAD_DIM), _F32)],
        compiler_params=pltpu.CompilerParams(
            dimension_semantics=("arbitrary", "arbitrary"), vmem_limit_bytes=_V7X_VMEM_LIMIT_BYTES),
        name="layer_prompt",
    )(x, *_rope_tables(np.arange(l)), mk16, mv16, *consts, w_out)


def _layer_sample(x, cache_k, cache_v, s0, g_norm, w_in, weights, *, sp):
    n, ts, _ = x.shape
    m = n * ts
    n_slabs = IN_WIDTH // W_SLAB
    branch_step = lambda i: jnp.maximum(i - n_slabs, 0)
    per_step = lambda shape: pl.BlockSpec(shape, lambda i: (branch_step(i),) + (0,) * (len(shape) - 1))
    slab_spec = pl.BlockSpec((D_MODEL, W_SLAB), lambda i: (0, jnp.minimum(i, n_slabs - 1)))
    state_shape = (sp, RET_HEADS, HEAD_DIM, HEAD_DIM)
    kv_spec = pl.BlockSpec(memory_space=pl.ANY)
    kv_buf = pltpu.VMEM((2, sp * XA_HEADS, MEM_LEN, HEAD_DIM), _F32)
    consts = _branch_consts(weights, ts)
    w_in16, mix, s_new, gvn = pl.pallas_call(
        functools.partial(_sample_kernel, ts=ts, sp=sp, n_slabs=n_slabs, n_branch=n // sp),
        grid=(n_slabs + n // sp,),
        in_specs=[_const_spec((m, D_MODEL)), _const_spec((m, HEAD_DIM)), _const_spec((m, HEAD_DIM)), kv_spec, kv_spec,
                  per_step(state_shape), _const_spec((1, D_MODEL)), slab_spec]
        + [_const_spec(c.shape) for c in consts],
        out_specs=[slab_spec, pl.BlockSpec((m, MIX_WIDTH), lambda i: (0, 0)), per_step(state_shape),
                   pl.BlockSpec((m, GMLP_WIDTH), lambda i: (0, 0))],
        out_shape=[jax.ShapeDtypeStruct(w_in.shape, _BF16),
                   jax.ShapeDtypeStruct((m, MIX_WIDTH), _BF16),
                   jax.ShapeDtypeStruct((n,) + state_shape[1:], _F32),
                   jax.ShapeDtypeStruct((m, GMLP_WIDTH), _F32)],
        scratch_shapes=[pltpu.VMEM((m, D_MODEL), _BF16), pltpu.VMEM((n_slabs, m, W_SLAB), _F32),
                        pltpu.VMEM((GMLP_GROUPS, ts, HEAD_DIM), _F32), kv_buf, kv_buf,
                        pltpu.SemaphoreType.DMA((2, 2))],
        compiler_params=pltpu.CompilerParams(
            dimension_semantics=("arbitrary",), vmem_limit_bytes=_V7X_VMEM_LIMIT_BYTES),
        name="layer_sample",
    )(x.reshape(m, D_MODEL), *_rope_tables(np.tile(PAST_LEN + np.arange(ts), n)), cache_k, cache_v, s0,
      g_norm.reshape(1, -1), w_in, *consts)
    return w_in16, mix, s_new, gvn.reshape(n, ts, GMLP_WIDTH)


def _out_proj(x_p, mix_p, x_s, mix_s, w_out16, g_final, *, tile_rows):
    n_p = x_p.shape[0] // tile_rows
    p_blk = pl.BlockSpec((tile_rows, D_MODEL), lambda n: (jnp.minimum(n, n_p - 1), 0))
    s_out_blk = pl.BlockSpec(x_s.shape, lambda n: (0, 0))
    return pl.pallas_call(
        functools.partial(_outproj_kernel, n_prompt_tiles=n_p),
        grid=(n_p + 1,),
        in_specs=[p_blk, p_blk, pl.BlockSpec(memory_space=pl.ANY), pl.BlockSpec(memory_space=pl.ANY),
                  _const_spec(w_out16.shape), _const_spec((1, D_MODEL))],
        out_specs=[p_blk, s_out_blk],
        out_shape=[jax.ShapeDtypeStruct(x_p.shape, _F32), jax.ShapeDtypeStruct(x_s.shape, _F32)],
        scratch_shapes=[pltpu.VMEM(x_s.shape, _F32), pltpu.VMEM(mix_s.shape, _BF16), pltpu.SemaphoreType.DMA((2,))],
        compiler_params=pltpu.CompilerParams(
            dimension_semantics=("arbitrary",), vmem_limit_bytes=_V7X_VMEM_LIMIT_BYTES),
        name="out_proj",
    )(x_p, mix_p, x_s, mix_s, w_out16, g_final.reshape(1, -1))


def _memory_kv(mem, g_mem, w_mem_kv):
    b = mem.shape[0]
    nb = MEMKV_STREAMS
    blk = lambda width: pl.BlockSpec((nb, MEM_LEN, width), lambda i: (i, 0, 0))
    blk4 = pl.BlockSpec((nb, MEM_LEN, XA_HEADS, HEAD_DIM), lambda i: (i, 0, 0, 0))
    kv_f32 = jax.ShapeDtypeStruct((b, MEM_LEN, XA_HEADS, HEAD_DIM), _F32)
    kv_b16 = jax.ShapeDtypeStruct((b, MEM_LEN, XA_WIDTH), _BF16)
    return pl.pallas_call(
        _memkv_kernel,
        grid=(b // nb,),
        in_specs=[blk(D_MODEL), _const_spec((1, D_MODEL)), pl.BlockSpec(memory_space=pl.ANY)],
        out_specs=[blk4, blk4, blk(XA_WIDTH), blk(XA_WIDTH)],
        out_shape=[kv_f32, kv_f32, kv_b16, kv_b16],
        scratch_shapes=[pltpu.VMEM(w_mem_kv.shape, _F32), pltpu.VMEM(w_mem_kv.shape, _BF16),
                        pltpu.SemaphoreType.DMA((2 * XA_WIDTH // MEMKV_CHUNK,))],
        compiler_params=pltpu.CompilerParams(dimension_semantics=("arbitrary",)),
        name="memory_kv",
    )(mem, g_mem.reshape(1, -1), w_mem_kv)


MEMKV_STREAMS = 2
MEMKV_CHUNK = 256
PROMPT_TILE = 512
RET_BLOCK = 256
OUT_TILE = 512
SAMPLE_STEP = 4


def kernel(x_prompt, x_sample, mem_prompt, state_ret, cache_mem_k, cache_mem_v, g_norm, w_in, g_ret, g_gmlp,
           w_s, b_s, g_mem, w_mem_kv, w_out, g_final):
    assert g_norm.shape[0] == 1
    b_p, l_p, _ = x_prompt.shape
    b_s_, l_s, _ = x_sample.shape
    weights = (g_ret[0], g_gmlp[0], w_s[0], b_s[0])

    mk, mv, mk16, mv16 = _memory_kv(mem_prompt, g_mem[0], w_mem_kv[0])

    w_in16, mix_s, s_s, gvn_s = _layer_sample(
        x_sample, cache_mem_k[0], cache_mem_v[0], state_ret[0], g_norm[0], w_in[0], weights, sp=SAMPLE_STEP)

    mix_p, s_p, w_out16 = _layer_prompt(x_prompt, mk16, mv16, g_norm[0], w_in16, w_out[0], weights,
                                        ts=PROMPT_TILE, lb=RET_BLOCK)
    y_p, y_s = _out_proj(x_prompt.reshape(b_p * l_p, D_MODEL), mix_p.reshape(b_p * l_p, MIX_WIDTH),
                         x_sample.reshape(b_s_ * l_s, D_MODEL), mix_s, w_out16, g_final, tile_rows=OUT_TILE)

    return (y_p.reshape(x_prompt.shape), y_s.reshape(x_sample.shape), s_p[None], mk[None], mv[None],
            s_s[None], gvn_s[None])
```

```python
import functools

import numpy as np

import jax
import jax.numpy as jnp
from jax import lax
from jax.experimental import pallas as pl
from jax.experimental.pallas import tpu as pltpu

D_MODEL = 2048
PAST_LEN = 1024
MEM_LEN = 256
RET_HEADS = 8
HEAD_DIM = 128
RET_WIDTH = RET_HEADS * HEAD_DIM
GMLP_GROUPS = 4
GMLP_WIDTH = GMLP_GROUPS * HEAD_DIM
GMLP_CHUNK = 128
XA_HEADS = 4
XA_WIDTH = XA_HEADS * HEAD_DIM
MIX_WIDTH = RET_WIDTH + GMLP_WIDTH + XA_WIDTH
ROPE_BASE = 10000.0
EPS = 1e-6
_EARLY_OPERAND_SCALE = 2.0 ** -32

_RQ, _RK, _RV, _RG = 0, RET_WIDTH, 2 * RET_WIDTH, 3 * RET_WIDTH
_GU = 4 * RET_WIDTH
_GV = _GU + GMLP_WIDTH
_GG = _GV + GMLP_WIDTH
_AQ = _GG + GMLP_WIDTH
_AG = _AQ + XA_WIDTH
IN_WIDTH = _AG + XA_WIDTH
W_SLAB = 4 * HEAD_DIM
WOUT_SLAB = HEAD_DIM
_MIX_GMLP = RET_WIDTH
_MIX_XA = RET_WIDTH + GMLP_WIDTH

_V7X_VMEM_LIMIT_BYTES = 60 * 1024 * 1024

_BF16 = jnp.bfloat16
_F32 = jnp.float32


def _dot(a, b):
    return jnp.dot(a, b, preferred_element_type=_F32)


def _dot_nt(a, b):
    return lax.dot_general(a, b, (((1,), (1,)), ((), ())), preferred_element_type=_F32)


def _dot_tn(a, b):
    return lax.dot_general(a, b, (((0,), (0,)), ((), ())), preferred_element_type=_F32)


def _silu(x):
    return x / (1.0 + jnp.exp(-x))


def _rms_scale(x):
    return x * lax.rsqrt(jnp.mean(x * x, axis=-1, keepdims=True) + EPS)


def _center_scale(x):
    mu = jnp.mean(x, axis=-1, keepdims=True)
    d = x - mu
    return d * lax.rsqrt(jnp.mean(d * d, axis=-1, keepdims=True) + EPS)


def _rope(u, cos2, sin2):
    return u * cos2 + pltpu.roll(u, HEAD_DIM // 2, 1) * sin2


def _cols(base, i):
    return slice(base + i * HEAD_DIM, base + (i + 1) * HEAD_DIM)


def _retention_blocks(blocks, dmask_ref, qdec_ref, kdec_ref, sdec_ref):
    sc = [_dot_nt(qb, kb.astype(_BF16)) for qb, kb, _, _, _ in blocks]
    cross = [_dot(qb, state.astype(_BF16)) for qb, _, _, state, _ in blocks]
    new_states = [state * sdec_ref[head] + _dot_tn((kb * kdec_ref[head]).astype(_BF16), vb)
                  for _, kb, vb, state, head in blocks]
    o = [_dot((s * dmask_ref[head]).astype(_BF16), vb) + c * qdec_ref[head]
         for s, c, (_, _, vb, _, head) in zip(sc, cross, blocks)]
    return o, new_states


def _fill_row_bias(dst_ref, bs_ref, n):
    row = lax.broadcasted_iota(jnp.int32, (n, n), 0)
    col = lax.broadcasted_iota(jnp.int32, (n, n), 1)
    for g in range(GMLP_GROUPS):
        diag = jnp.where(row == col, jnp.broadcast_to(bs_ref[g:g + 1, :n], (n, n)), 0.0)
        dst_ref[g] = jnp.broadcast_to(jnp.sum(diag, axis=1, keepdims=True), (n, HEAD_DIM))


def _tril_bf16(w):
    n = w.shape[0]
    row = lax.broadcasted_iota(jnp.int32, (n, n), 0)
    col = lax.broadcasted_iota(jnp.int32, (n, n), 1)
    return jnp.where(row >= col, w, 0.0).astype(_BF16)


def _xattn_heads(heads):
    sc = [_dot_nt(aq.astype(_BF16), mk) * (HEAD_DIM ** -0.5) for aq, mk, _ in heads]
    e = [jnp.exp(s - jnp.max(s, axis=-1, keepdims=True)) for s in sc]
    return [_dot(p.astype(_BF16), mv) / jnp.sum(p, axis=-1, keepdims=True) for p, (_, _, mv) in zip(e, heads)]


def _memkv_kernel(mem_ref, g_ref, w_hbm, k_ref, v_ref, k16_ref, v16_ref, wf_s, w16_s, w_sem):
    nb = mem_ref.shape[0]
    n_chunks = D_MODEL // MEMKV_CHUNK
    chunk = lambda c: slice(c * MEMKV_CHUNK, (c + 1) * MEMKV_CHUNK)
    copies = [pltpu.make_async_copy(w_hbm.at[chunk(c)], wf_s.at[chunk(c)], w_sem.at[c]) for c in range(n_chunks)]

    def body(first):
        if first:
            for copy in copies:
                copy.start()
        n = (_rms_scale(mem_ref[...].reshape(nb * MEM_LEN, D_MODEL)) * g_ref[...]).astype(_BF16)
        if first:
            kv = None
            for c in range(n_chunks):
                copies[c].wait()
                w16_s[chunk(c), :] = wf_s[chunk(c), :].astype(_BF16)
                part = _dot(n[:, chunk(c)], w16_s[chunk(c), :])
                kv = part if kv is None else kv + part
        else:
            kv = _dot(n, w16_s[...])
        for i in range(nb):
            k = kv[i * MEM_LEN:(i + 1) * MEM_LEN, :XA_WIDTH]
            v = kv[i * MEM_LEN:(i + 1) * MEM_LEN, XA_WIDTH:]
            for hd in range(XA_HEADS):
                k_ref[i, :, hd, :] = k[:, _cols(0, hd)]
                v_ref[i, :, hd, :] = v[:, _cols(0, hd)]
            k16_ref[i] = k.astype(_BF16)
            v16_ref[i] = v.astype(_BF16)

    @pl.when(pl.program_id(0) == 0)
    def _():
        body(True)

    @pl.when(pl.program_id(0) > 0)
    def _():
        body(False)


def _prompt_kernel(x_ref, cos_ref, sin_ref, mk_ref, mv_ref, gnorm_ref, win_ref, gret_ref, ggm_ref, ws_ref, bs_ref,
                   dmask_ref, qdec_ref, kdec_ref, sdec_ref, wout_ref, mix_ref, s_ref, wout16_ref, h_s, bias_s,
                   *, lb, n_out_slabs):
    ts = x_ref.shape[1]
    step = pl.program_id(0) * pl.num_programs(1) + pl.program_id(1)

    @pl.when(step == 0)
    def _():
        _fill_row_bias(bias_s, bs_ref, GMLP_CHUNK)

    @pl.when(step < n_out_slabs)
    def _():
        wout16_ref[...] = wout_ref[...].astype(_BF16)

    @pl.when(pl.program_id(1) == 0)
    def _():
        s_ref[...] = jnp.zeros_like(s_ref)

    x = x_ref[0]
    h_s[...] = (x * (gnorm_ref[...] * _EARLY_OPERAND_SCALE)).astype(_BF16)
    row_scale = lax.rsqrt(jnp.mean(x * x, axis=-1, keepdims=True) + EPS)
    row_scale_b = jnp.broadcast_to(row_scale * (1.0 / _EARLY_OPERAND_SCALE), (ts, HEAD_DIM))

    def proj(c0, width, scaled_rows=True):
        raw = _dot(h_s[...], win_ref[:, c0:c0 + width])
        if scaled_rows:
            return raw
        return jnp.concatenate([raw[:, _cols(0, i)] * row_scale_b for i in range(width // HEAD_DIM)], axis=1)

    cos = cos_ref[...]
    sin = sin_ref[...]

    decay = (dmask_ref, qdec_ref, kdec_ref, sdec_ref)

    def retention_pair(p):
        c0 = 2 * HEAD_DIM * p
        q2, k2, v2, g2 = (proj(sec + c0, 2 * HEAD_DIM, scaled_rows=p > 0) for sec in (_RQ, _RK, _RV, _RG))
        if p == 0:
            h_s[...] = (x_ref[0] * row_scale * gnorm_ref[...]).astype(_BF16)
        heads = (2 * p, 2 * p + 1)
        qh = [_rope(q2[:, _cols(0, hh)], cos, sin).astype(_BF16) for hh in range(2)]
        kh = [_rope(k2[:, _cols(0, hh)], cos, sin) * (HEAD_DIM ** -0.5) for hh in range(2)]
        vh = [v2[:, _cols(0, hh)].astype(_BF16) for hh in range(2)]
        for bi in range(ts // lb):
            rs = slice(bi * lb, (bi + 1) * lb)
            o, new_states = _retention_blocks(
                [(qh[hh][rs], kh[hh][rs], vh[hh][rs], s_ref[0, head], head) for hh, head in enumerate(heads)],
                *decay)
            for hh, head in enumerate(heads):
                s_ref[0, head] = new_states[hh]
                on = _center_scale(o[hh]) * gret_ref[:, _cols(0, head)]
                mix_ref[0, rs, _cols(0, head)] = (on * _silu(g2[rs, _cols(0, hh)])).astype(_BF16)

    def gmlp():
        gvn = _center_scale(proj(_GV, GMLP_WIDTH)) * ggm_ref[...]
        gg = proj(_GG, GMLP_WIDTH)
        gu = proj(_GU, GMLP_WIDTH)
        for g in range(GMLP_GROUPS):
            hs = _cols(0, g)
            wm = _tril_bf16(ws_ref[g])
            for c in range(ts // GMLP_CHUNK):
                rs = slice(c * GMLP_CHUNK, (c + 1) * GMLP_CHUNK)
                sg = _dot(wm, gvn[rs, hs].astype(_BF16)) + bias_s[g]
                mix_ref[0, rs, _cols(_MIX_GMLP, g)] = (gu[rs, hs] * (sg * _silu(gg[rs, hs]))).astype(_BF16)

    def xattn():
        aq = proj(_AQ, XA_WIDTH)
        ag = proj(_AG, XA_WIDTH)
        ao = _xattn_heads([(aq[:, _cols(0, hd)], mk_ref[0, :, _cols(0, hd)], mv_ref[0, :, _cols(0, hd)])
                           for hd in range(XA_HEADS)])
        for hd in range(XA_HEADS):
            mix_ref[0, :, _cols(_MIX_XA, hd)] = (ao[hd] * _silu(ag[:, _cols(0, hd)])).astype(_BF16)

    for p in range(RET_HEADS // 2):
        retention_pair(p)
    xattn()
    gmlp()


class _SlabView:
    def __init__(self, ref):
        self.ref = ref

    def _at(self, rows, base, i):
        col = base + i * HEAD_DIM
        return (col // W_SLAB, rows, slice(col % W_SLAB, col % W_SLAB + HEAD_DIM))

    def get(self, rows, base, i):
        return self.ref[self._at(rows, base, i)]

    def set(self, rows, base, i, value):
        self.ref[self._at(rows, base, i)] = value


def _sample_kernel(x_ref, cos_ref, sin_ref, ck_hbm, cv_hbm, s0_ref, gnorm_ref, w_ref, gret_ref, ggm_ref, ws_ref,
                   bs_ref, dmask_ref, qdec_ref, kdec_ref, sdec_ref, w16_ref, mix_ref, s_ref, gvn_ref, h_s, proj_s,
                   bias_s, kbuf, vbuf, kv_sem, *, ts, sp, n_slabs, n_branch):
    n = pl.program_id(0)
    proj = _SlabView(proj_s)
    all_rows = slice(None)

    def kv_copies(j, slot):
        copies = []
        for i in range(sp):
            for hd in range(XA_HEADS):
                for a, (src, dst) in enumerate(((ck_hbm, kbuf), (cv_hbm, vbuf))):
                    copies.append(pltpu.make_async_copy(
                        src.at[j * sp + i, :, hd, :], dst.at[slot, i * XA_HEADS + hd], kv_sem.at[slot, a]))
        return copies

    @pl.when(n == 0)
    def _():
        for copy in kv_copies(0, 0):
            copy.start()
        h_s[...] = (_rms_scale(x_ref[...]) * gnorm_ref[...]).astype(_BF16)
        _fill_row_bias(bias_s, bs_ref, ts)

    @pl.when(n < n_slabs)
    def _():
        w16 = w_ref[...].astype(_BF16)
        w16_ref[...] = w16
        proj_s[n] = _dot(h_s[...], w16)

    @pl.when(n == n_slabs - 1)
    def _():
        cos = cos_ref[...]
        sin = sin_ref[...]
        for head in range(RET_HEADS):
            proj.set(all_rows, _RQ, head, _rope(proj.get(all_rows, _RQ, head), cos, sin))
            proj.set(all_rows, _RK, head, _rope(proj.get(all_rows, _RK, head), cos, sin) * (HEAD_DIM ** -0.5))
        gv = jnp.concatenate([proj.get(all_rows, _GV, g) for g in range(GMLP_GROUPS)], axis=1)
        gvn = _center_scale(gv) * ggm_ref[...]
        gvn_ref[...] = gvn
        for g in range(GMLP_GROUPS):
            proj.set(all_rows, _GV, g, gvn[:, _cols(0, g)])

    @pl.when(n >= n_slabs)
    def _():
        j = n - n_slabs
        slot = j % 2
        rows = [pl.ds(pl.multiple_of((j * sp + i) * ts, ts), ts) for i in range(sp)]

        @pl.when(j + 1 < n_branch)
        def _():
            for copy in kv_copies(j + 1, 1 - slot):
                copy.start()

        for copy in kv_copies(j, slot):
            copy.wait()

        units = [(i, head) for i in range(sp) for head in range(RET_HEADS)]
        o, new_states = _retention_blocks(
            [(proj.get(rows[i], _RQ, head).astype(_BF16), proj.get(rows[i], _RK, head),
              proj.get(rows[i], _RV, head).astype(_BF16), s0_ref[i, head], head) for i, head in units],
            dmask_ref, qdec_ref, kdec_ref, sdec_ref)
        for (i, head), o_u, s_u in zip(units, o, new_states):
            s_ref[i, head] = s_u
            on = _center_scale(o_u) * gret_ref[:, _cols(0, head)]
            mix_ref[rows[i], _cols(0, head)] = (on * _silu(proj.get(rows[i], _RG, head))).astype(_BF16)

        for g in range(GMLP_GROUPS):
            wm = _tril_bf16(ws_ref[g, :ts, :ts])
            for i in range(sp):
                sg = _dot(wm, proj.get(rows[i], _GV, g).astype(_BF16)) + bias_s[g]
                mix_ref[rows[i], _cols(_MIX_GMLP, g)] = (
                    proj.get(rows[i], _GU, g) * sg * _silu(proj.get(rows[i], _GG, g))).astype(_BF16)

        units = [(i, hd) for i in range(sp) for hd in range(XA_HEADS)]
        ao = _xattn_heads([(proj.get(rows[i], _AQ, hd), kbuf[slot, i * XA_HEADS + hd].astype(_BF16),
                            vbuf[slot, i * XA_HEADS + hd].astype(_BF16)) for i, hd in units])
        for (i, hd), ao_u in zip(units, ao):
            mix_ref[rows[i], _cols(_MIX_XA, hd)] = (ao_u * _silu(proj.get(rows[i], _AG, hd))).astype(_BF16)


def _outproj_kernel(xp_ref, mixp_ref, xs_hbm, mixs_hbm, w_ref, g_ref, yp_ref, ys_ref, xs_buf, mixs_buf, s_sem,
                    *, n_prompt_tiles):
    n = pl.program_id(0)
    sample_copies = (pltpu.make_async_copy(xs_hbm, xs_buf, s_sem.at[0]),
                     pltpu.make_async_copy(mixs_hbm, mixs_buf, s_sem.at[1]))

    def project(x_ref, mix_ref, y_ref):
        y = x_ref[...] + _dot(mix_ref[...], w_ref[...])
        y_ref[...] = _rms_scale(y) * g_ref[...]

    @pl.when(n == 0)
    def _():
        for copy in sample_copies:
            copy.start()

    @pl.when(n < n_prompt_tiles)
    def _():
        project(xp_ref, mixp_ref, yp_ref)

    @pl.when(n == n_prompt_tiles)
    def _():
        for copy in sample_copies:
            copy.wait()
        project(xs_buf, mixs_buf, ys_ref)


def _const_spec(shape):
    return pl.BlockSpec(shape, lambda *_: (0,) * len(shape), pipeline_mode=pl.Buffered(1))


def _rope_tables(pos):
    inv_freq = ROPE_BASE ** (-np.arange(0, HEAD_DIM, 2, dtype=np.float64) / HEAD_DIM)
    ang = np.asarray(pos, np.float64)[:, None] * inv_freq[None, :]
    cos, sin = np.cos(ang), np.sin(ang)
    return (np.concatenate([cos, cos], axis=-1).astype(np.float32),
            np.concatenate([-sin, sin], axis=-1).astype(np.float32))


def _decay_tables(lb):
    log_gamma = np.log(1.0 - 2.0 ** (-5.0 - np.arange(RET_HEADS, dtype=np.float64)))
    idx = np.arange(lb, dtype=np.float64)
    diff = idx[:, None] - idx[None, :]
    dmask = np.where(diff >= 0, np.exp(log_gamma[:, None, None] * np.maximum(diff, 0.0)[None]), 0.0)
    qdec = np.exp(log_gamma[:, None] * (idx[None, :] + 1.0))
    kdec = np.exp(log_gamma[:, None] * (lb - 1.0 - idx[None, :]))
    sdec = np.exp(log_gamma * lb)
    bcast = lambda a: np.broadcast_to(a[..., None], a.shape + (HEAD_DIM,)).astype(np.float32)
    return [dmask.astype(np.float32), bcast(qdec), bcast(kdec), bcast(sdec)[:, None, :]]


def _branch_consts(weights, lb):
    g_ret, g_gmlp, w_s, b_s = weights
    return [g_ret.reshape(1, -1), g_gmlp.reshape(1, -1), w_s, b_s] + _decay_tables(lb)


def _layer_prompt(x, mk16, mv16, g_norm, w_in16, w_out, weights, *, ts, lb):
    nb, l, _ = x.shape
    nt = l // ts
    n_out_slabs = D_MODEL // WOUT_SLAB
    assert n_out_slabs <= nb * nt
    tile = lambda width: pl.BlockSpec((1, ts, width), lambda b, t: (b, t, 0))
    per_stream = lambda shape: pl.BlockSpec(shape, lambda b, t: (b,) + (0,) * (len(shape) - 1))
    rope_spec = pl.BlockSpec((ts, HEAD_DIM), lambda b, t: (t, 0))
    out_slab_spec = pl.BlockSpec((MIX_WIDTH, WOUT_SLAB), lambda b, t: (0, jnp.minimum(b * nt + t, n_out_slabs - 1)))
    state_shape = (1, RET_HEADS, HEAD_DIM, HEAD_DIM)
    kv_spec = per_stream((1, MEM_LEN, XA_WIDTH))
    consts = [g_norm.reshape(1, -1), w_in16] + _branch_consts(weights, lb)
    return pl.pallas_call(
        functools.partial(_prompt_kernel, lb=lb, n_out_slabs=n_out_slabs),
        grid=(nb, nt),
        in_specs=[tile(D_MODEL), rope_spec, rope_spec, kv_spec, kv_spec] + [_const_spec(c.shape) for c in consts]
        + [out_slab_spec],
        out_specs=[tile(MIX_WIDTH), per_stream(state_shape), out_slab_spec],
        out_shape=[jax.ShapeDtypeStruct((nb, l, MIX_WIDTH), _BF16),
                   jax.ShapeDtypeStruct((nb,) + state_shape[1:], _F32),
                   jax.ShapeDtypeStruct(w_out.shape, _BF16)],
        scratch_shapes=[pltpu.VMEM((ts, D_MODEL), _BF16), pltpu.VMEM((GMLP_GROUPS, GMLP_CHUNK, HEAD_DIM), _F32)],
        compiler_params=pltpu.CompilerParams(
            dimension_semantics=("arbitrary", "arbitrary"), vmem_limit_bytes=_V7X_VMEM_LIMIT_BYTES),
        name="layer_prompt",
    )(x, *_rope_tables(np.arange(l)), mk16, mv16, *consts, w_out)


def _layer_sample(x, cache_k, cache_v, s0, g_norm, w_in, weights, *, sp):
    n, ts, _ = x.shape
    m = n * ts
    n_slabs = IN_WIDTH // W_SLAB
    branch_step = lambda i: jnp.maximum(i - n_slabs, 0)
    per_step = lambda shape: pl.BlockSpec(shape, lambda i: (branch_step(i),) + (0,) * (len(shape) - 1))
    slab_spec = pl.BlockSpec((D_MODEL, W_SLAB), lambda i: (0, jnp.minimum(i, n_slabs - 1)))
    state_shape = (sp, RET_HEADS, HEAD_DIM, HEAD_DIM)
    kv_spec = pl.BlockSpec(memory_space=pl.ANY)
    kv_buf = pltpu.VMEM((2, sp * XA_HEADS, MEM_LEN, HEAD_DIM), _F32)
    consts = _branch_consts(weights, ts)
    w_in16, mix, s_new, gvn = pl.pallas_call(
        functools.partial(_sample_kernel, ts=ts, sp=sp, n_slabs=n_slabs, n_branch=n // sp),
        grid=(n_slabs + n // sp,),
        in_specs=[_const_spec((m, D_MODEL)), _const_spec((m, HEAD_DIM)), _const_spec((m, HEAD_DIM)), kv_spec, kv_spec,
                  per_step(state_shape), _const_spec((1, D_MODEL)), slab_spec]
        + [_const_spec(c.shape) for c in consts],
        out_specs=[slab_spec, pl.BlockSpec((m, MIX_WIDTH), lambda i: (0, 0)), per_step(state_shape),
                   pl.BlockSpec((m, GMLP_WIDTH), lambda i: (0, 0))],
        out_shape=[jax.ShapeDtypeStruct(w_in.shape, _BF16),
                   jax.ShapeDtypeStruct((m, MIX_WIDTH), _BF16),
                   jax.ShapeDtypeStruct((n,) + state_shape[1:], _F32),
                   jax.ShapeDtypeStruct((m, GMLP_WIDTH), _F32)],
        scratch_shapes=[pltpu.VMEM((m, D_MODEL), _BF16), pltpu.VMEM((n_slabs, m, W_SLAB), _F32),
                        pltpu.VMEM((GMLP_GROUPS, ts, HEAD_DIM), _F32), kv_buf, kv_buf,
                        pltpu.SemaphoreType.DMA((2, 2))],
        compiler_params=pltpu.CompilerParams(
            dimension_semantics=("arbitrary",), vmem_limit_bytes=_V7X_VMEM_LIMIT_BYTES),
        name="layer_sample",
    )(x.reshape(m, D_MODEL), *_rope_tables(np.tile(PAST_LEN + np.arange(ts), n)), cache_k, cache_v, s0,
      g_norm.reshape(1, -1), w_in, *consts)
    return w_in16, mix, s_new, gvn.reshape(n, ts, GMLP_WIDTH)


def _out_proj(x_p, mix_p, x_s, mix_s, w_out16, g_final, *, tile_rows):
    n_p = x_p.shape[0] // tile_rows
    p_blk = pl.BlockSpec((tile_rows, D_MODEL), lambda n: (jnp.minimum(n, n_p - 1), 0))
    s_out_blk = pl.BlockSpec(x_s.shape, lambda n: (0, 0))
    return pl.pallas_call(
        functools.partial(_outproj_kernel, n_prompt_tiles=n_p),
        grid=(n_p + 1,),
        in_specs=[p_blk, p_blk, pl.BlockSpec(memory_space=pl.ANY), pl.BlockSpec(memory_space=pl.ANY),
                  _const_spec(w_out16.shape), _const_spec((1, D_MODEL))],
        out_specs=[p_blk, s_out_blk],
        out_shape=[jax.ShapeDtypeStruct(x_p.shape, _F32), jax.ShapeDtypeStruct(x_s.shape, _F32)],
        scratch_shapes=[pltpu.VMEM(x_s.shape, _F32), pltpu.VMEM(mix_s.shape, _BF16), pltpu.SemaphoreType.DMA((2,))],
        compiler_params=pltpu.CompilerParams(
            dimension_semantics=("arbitrary",), vmem_limit_bytes=_V7X_VMEM_LIMIT_BYTES),
        name="out_proj",
    )(x_p, mix_p, x_s, mix_s, w_out16, g_final.reshape(1, -1))


def _memory_kv(mem, g_mem, w_mem_kv):
    b = mem.shape[0]
    nb = MEMKV_STREAMS
    blk = lambda width: pl.BlockSpec((nb, MEM_LEN, width), lambda i: (i, 0, 0))
    blk4 = pl.BlockSpec((nb, MEM_LEN, XA_HEADS, HEAD_DIM), lambda i: (i, 0, 0, 0))
    kv_f32 = jax.ShapeDtypeStruct((b, MEM_LEN, XA_HEADS, HEAD_DIM), _F32)
    kv_b16 = jax.ShapeDtypeStruct((b, MEM_LEN, XA_WIDTH), _BF16)
    return pl.pallas_call(
        _memkv_kernel,
        grid=(b // nb,),
        in_specs=[blk(D_MODEL), _const_spec((1, D_MODEL)), pl.BlockSpec(memory_space=pl.ANY)],
        out_specs=[blk4, blk4, blk(XA_WIDTH), blk(XA_WIDTH)],
        out_shape=[kv_f32, kv_f32, kv_b16, kv_b16],
        scratch_shapes=[pltpu.VMEM(w_mem_kv.shape, _F32), pltpu.VMEM(w_mem_kv.shape, _BF16),
                        pltpu.SemaphoreType.DMA((D_MODEL // MEMKV_CHUNK,))],
        compiler_params=pltpu.CompilerParams(dimension_semantics=("arbitrary",)),
        name="memory_kv",
    )(mem, g_mem.reshape(1, -1), w_mem_kv)


MEMKV_STREAMS = 2
MEMKV_CHUNK = 512
PROMPT_TILE = 512
RET_BLOCK = 256
OUT_TILE = 512
SAMPLE_STEP = 4


def kernel(x_prompt, x_sample, mem_prompt, state_ret, cache_mem_k, cache_mem_v, g_norm, w_in, g_ret, g_gmlp,
           w_s, b_s, g_mem, w_mem_kv, w_out, g_final):
    assert g_norm.shape[0] == 1
    b_p, l_p, _ = x_prompt.shape
    b_s_, l_s, _ = x_sample.shape
    weights = (g_ret[0], g_gmlp[0], w_s[0], b_s[0])

    mk, mv, mk16, mv16 = _memory_kv(mem_prompt, g_mem[0], w_mem_kv[0])

    w_in16, mix_s, s_s, gvn_s = _layer_sample(
        x_sample, cache_mem_k[0], cache_mem_v[0], state_ret[0], g_norm[0], w_in[0], weights, sp=SAMPLE_STEP)

    mix_p, s_p, w_out16 = _layer_prompt(x_prompt, mk16, mv16, g_norm[0], w_in16, w_out[0], weights,
                                        ts=PROMPT_TILE, lb=RET_BLOCK)
    y_p, y_s = _out_proj(x_prompt.reshape(b_p * l_p, D_MODEL), mix_p.reshape(b_p * l_p, MIX_WIDTH),
                         x_sample.reshape(b_s_ * l_s, D_MODEL), mix_s, w_out16, g_final, tile_rows=OUT_TILE)

    return (y_p.reshape(x_prompt.shape), y_s.reshape(x_sample.shape), s_p[None], mk[None], mv[None],
            s_s[None], gvn_s[None])
```
